```python
import functools
import jax, jax.numpy as jnp
from jax import lax
import numpy as np

D_MODEL = 1024
BATCH = 32
SEQ = 2048
DEPTH = 4

CTX_LEN = 256
GRID_W = 64

HA_HEADS = 4
HA_DK = 128
HA_DV = 128
RB_HEADS = 4
RB_DK = 128
RB_DV = 128
GC_HEADS = 4
GC_DK = 128
GC_DV = 256
GC_RANK = 16
GC_TAU = 16.0
N_EXPERTS = 16
EXPERT_FF = 2816
EC_CAPACITY_FACTOR = 2

CHUNK = 64
SUB = 16
N_SUB = CHUNK // SUB
ROPE_BASE = 10000.0
LN_EPS = 1e-5
RMS_EPS = 1e-6
LB_FLOOR = 1e-30
DN_ALPHA = (2 * DEPTH) ** 0.25
DN_BETA = (8 * DEPTH) ** -0.25
N_EVEN = (DEPTH + 1) // 2
N_ODD = DEPTH // 2

HA_KEY = HA_HEADS * HA_DK
HA_VAL = HA_HEADS * HA_DV
RB_KEY = RB_HEADS * RB_DK
RB_VAL = RB_HEADS * RB_DV
GC_KEY = GC_HEADS * GC_DK
GC_VAL = GC_HEADS * GC_DV
EVEN_SIZES = (HA_KEY, HA_KEY, HA_KEY, HA_VAL, HA_VAL, RB_KEY, RB_KEY, RB_VAL, RB_VAL)
EVEN_IN = sum(EVEN_SIZES)
EVEN_OUT = HA_VAL + RB_VAL
ODD_SIZES = (GC_KEY, GC_KEY, GC_VAL, GC_VAL, 2 * GC_RANK)
ODD_IN = sum(ODD_SIZES)
ODD_OUT = GC_VAL

kernel_name = 'hybrid_hgrn2_retnet_gla_ecmoe_diffusion'


def _split(p, sizes):
    return jnp.split(p, np.cumsum(sizes)[:-1].tolist(), axis=-1)


def _heads(t, n_heads):
    b, l, _ = t.shape
    return t.reshape(b, l, n_heads, -1).transpose(0, 2, 1, 3)


def _merge(t):
    b, h, l, d = t.shape
    return t.transpose(0, 2, 1, 3).reshape(b, l, h * d)


def _layer_norm(x, w, b):
    xf = x.astype(jnp.float32)
    mu = xf.mean(-1, keepdims=True)
    var = jnp.square(xf - mu).mean(-1, keepdims=True)
    y = (xf - mu) * lax.rsqrt(var + LN_EPS) * w.astype(jnp.float32) + b.astype(jnp.float32)
    return y.astype(x.dtype)


def _rms_norm(x, w=None):
    y = x * lax.rsqrt(jnp.mean(jnp.square(x), -1, keepdims=True) + RMS_EPS)
    return y if w is None else y * w.astype(jnp.float32)


def _axial_rope(rows):
    r_idx, c_idx = jnp.meshgrid(jnp.arange(rows), jnp.arange(GRID_W), indexing='ij')
    n_freq = RB_DK // 4
    freq = ROPE_BASE ** (-jnp.arange(n_freq, dtype=jnp.float32) / n_freq)
    ang = jnp.concatenate([r_idx.reshape(-1, 1).astype(jnp.float32) * freq,
                           c_idx.reshape(-1, 1).astype(jnp.float32) * freq], axis=-1)
    return jnp.cos(ang), jnp.sin(ang)


def _apply_rope(t, cos, sin):
    t1, t2 = jnp.split(t, 2, axis=-1)
    return jnp.concatenate([t1 * cos - t2 * sin, t1 * sin + t2 * cos], axis=-1)


def _state_pass(chunk_kv, chunk_decay, s0):
    def step(s, inp):
        kv, dec = inp
        return dec[..., None] * s + kv, s
    s_final, s_prev = lax.scan(step, s0, (jnp.moveaxis(chunk_kv, 2, 0), jnp.moveaxis(chunk_decay, 2, 0)))
    return jnp.moveaxis(s_prev, 0, 2), s_final


def _gated_chunk_scan(q, k, v, log_f, s0):
    b_, h_, l_, kd = q.shape
    vd = v.shape[-1]
    n = l_ // CHUNK
    q, k, log_f = [t.reshape(b_, h_, n, CHUNK, kd) for t in (q, k, log_f)]
    v = v.reshape(b_, h_, n, CHUNK, vd)
    cum = jnp.cumsum(log_f, axis=3)
    qs, ks, cs = [t.reshape(b_, h_, n, N_SUB, SUB, kd) for t in (q, k, cum)]
    blk_end = cs[..., -1, :]
    blk_start = jnp.concatenate([jnp.zeros_like(blk_end[:, :, :, :1]), blk_end[:, :, :, :-1]], axis=3)
    mid = cs[..., SUB // 2 - 1:SUB // 2, :]
    diag = jnp.einsum('bhnaik,bhnajk->bhnaij', qs * jnp.exp(cs - mid), ks * jnp.exp(mid - cs))
    diag = jnp.where(jnp.tril(jnp.ones((SUB, SUB), bool)), diag, 0.0)
    q_in = qs * jnp.exp(cs - blk_start[..., None, :])
    k_out = ks * jnp.exp(blk_end[..., None, :] - cs)
    cross_dec = jnp.exp(jnp.minimum(blk_start[:, :, :, :, None, :] - blk_end[:, :, :, None, :, :], 0.0))
    cross = jnp.einsum('bhnaik,bhnack,bhncjk->bhnaicj', q_in, cross_dec, k_out)
    blk_lower = jnp.arange(N_SUB)[:, None] > jnp.arange(N_SUB)[None, :]
    cross = jnp.where(blk_lower[:, None, :, None], cross, 0.0)
    scores = cross + jnp.einsum('bhnaij,ac->bhnaicj', diag, jnp.eye(N_SUB, dtype=diag.dtype))
    scores = scores.reshape(b_, h_, n, CHUNK, CHUNK)
    last = cum[:, :, :, -1:]
    chunk_kv = jnp.einsum('bhnjk,bhnjv->bhnkv', k * jnp.exp(last - cum), v)
    s_prev, s_final = _state_pass(chunk_kv, jnp.exp(last[:, :, :, 0]), s0)
    o = (jnp.einsum('bhnij,bhnjv->bhniv', scores, v)
         + jnp.einsum('bhnik,bhnkv->bhniv', q * jnp.exp(cum), s_prev))
    return o.reshape(b_, h_, l_, vd), s_final


def _retention_chunk_scan(log_gamma, q, k, v, s0):
    b_, h_, l_, kd = q.shape
    vd = v.shape[-1]
    n = l_ // CHUNK
    q, k = [t.reshape(b_, h_, n, CHUNK, kd) for t in (q, k)]
    v = v.reshape(b_, h_, n, CHUNK, vd)
    pos = jnp.arange(CHUNK, dtype=jnp.float32)
    diff = pos[:, None] - pos[None, :]
    decay = jnp.where(diff >= 0, jnp.exp(log_gamma[:, None, None] * jnp.maximum(diff, 0.0)), 0.0)
    scores = jnp.einsum('bhnik,bhnjk->bhnij', q, k) * decay[None, :, None]
    k_dec = jnp.exp(log_gamma[:, None] * (CHUNK - 1 - pos))[None, :, None, :, None]
    q_dec = jnp.exp(log_gamma[:, None] * (pos + 1))[None, :, None, :, None]
    chunk_kv = jnp.einsum('bhnjk,bhnjv->bhnkv', k * k_dec, v)
    chunk_decay = jnp.broadcast_to(jnp.exp(log_gamma * CHUNK)[None, :, None, None], (b_, h_, n, kd))
    s_prev, s_final = _state_pass(chunk_kv, chunk_decay, s0)
    o = (jnp.einsum('bhnij,bhnjv->bhniv', scores, v)
         + jnp.einsum('bhnik,bhnkv->bhniv', q * q_dec, s_prev))
    return o.reshape(b_, h_, l_, vd), s_final


def _bidirectional(scan_f, scan_b, args_f, args_b, s0):
    o_f, s_f = scan_f(*args_f, s0[0])
    o_b, s_b = scan_b(*[jnp.flip(a, axis=2) for a in args_b], s0[1])
    return o_f + jnp.flip(o_b, axis=2), (s_f, s_b)


def _hgrn2_forget(z, lb):
    log_f = jnp.logaddexp(z, jnp.log(jnp.maximum(lb, LB_FLOOR))) - jax.nn.softplus(z)
    key = (1.0 - lb) * jax.nn.sigmoid(-z)
    return log_f, key


def _even_mixer(h, w_in, w_out, lb, ha_norm_w, rb_log_gamma, rope, states):
    p = jnp.einsum('bld,de->ble', h, w_in).astype(jnp.float32)
    qa, fa_f, fa_b, ia, ga, qb, kb, vb, gb = _split(p, EVEN_SIZES)
    qa = _heads(qa, HA_HEADS) * HA_DK ** -0.5
    ia = _heads(ia, HA_HEADS)
    lb = lb.astype(jnp.float32).reshape(2, HA_HEADS, 1, HA_DK)
    lf_f, ka_f = _hgrn2_forget(_heads(fa_f, HA_HEADS), lb[0])
    lf_b, ka_b = _hgrn2_forget(_heads(fa_b, HA_HEADS), lb[1])
    oa, st_a = _bidirectional(_gated_chunk_scan, _gated_chunk_scan,
                              (qa, ka_f, ia, lf_f), (qa, ka_b, ia, lf_b), states[0])
    ya = _merge(_rms_norm(oa, ha_norm_w)) * jax.nn.silu(ga)
    qb = _heads(qb, RB_HEADS) * RB_DK ** -0.5
    kb = _heads(kb, RB_HEADS)
    if rope is not None:
        qb = _apply_rope(qb, *rope)
        kb = _apply_rope(kb, *rope)
    vb = _heads(vb, RB_HEADS)
    ob, st_b = _bidirectional(functools.partial(_retention_chunk_scan, rb_log_gamma[0]),
                              functools.partial(_retention_chunk_scan, rb_log_gamma[1]),
                              (qb, kb, vb), (qb, kb, vb), states[1])
    yb = _merge(_rms_norm(ob)) * jax.nn.silu(gb)
    y = jnp.concatenate([ya, yb], axis=-1).astype(h.dtype)
    return jnp.einsum('ble,ed->bld', y, w_out), (st_a, st_b)


def _odd_mixer(h, w_in, w_out, w2, b2, norm_w, states):
    p = jnp.einsum('bld,de->ble', h, w_in).astype(jnp.float32)
    qc, kc, vc, gc, lr = _split(p, ODD_SIZES)
    q = _heads(qc, GC_HEADS) * GC_DK ** -0.5
    k = _heads(kc, GC_HEADS)
    v = _heads(vc, GC_HEADS)
    lr_f, lr_b = jnp.split(lr, 2, axis=-1)
    w2 = w2.astype(jnp.float32)
    b2 = b2.astype(jnp.float32)
    lf_f = _heads(jax.nn.log_sigmoid(lr_f @ w2[0] + b2[0]) / GC_TAU, GC_HEADS)
    lf_b = _heads(jax.nn.log_sigmoid(lr_b @ w2[1] + b2[1]) / GC_TAU, GC_HEADS)
    o, st = _bidirectional(_gated_chunk_scan, _gated_chunk_scan,
                           (q, k, v, lf_f), (q, k, v, lf_b), states)
    y = (_merge(_rms_norm(o, norm_w)) * jax.nn.silu(gc)).astype(h.dtype)
    return jnp.einsum('ble,ed->bld', y, w_out), st


def _ec_ffn(h, w_router, w_gate, w_up, w_down):
    b, n, d = h.shape
    cap = EC_CAPACITY_FACTOR * n // N_EXPERTS
    aff = jax.nn.softmax(jnp.einsum('bnd,de->bne', h, w_router).astype(jnp.float32), axis=-1)
    gate, idx = lax.top_k(jnp.swapaxes(aff, 1, 2), cap)
    xs = jax.vmap(lambda hb, ib: hb[ib])(h, idx)
    hid = jax.nn.silu(jnp.einsum('becd,edf->becf', xs, w_gate)) * jnp.einsum('becd,edf->becf', xs, w_up)
    ys = jnp.einsum('becf,efd->becd', hid, w_down) * gate[..., None].astype(h.dtype)
    return jax.vmap(lambda yb, ib: jnp.zeros((n, d), h.dtype).at[ib.reshape(-1)].add(yb.reshape(-1, d)))(ys, idx)


def setup_inputs(seed: int = 0) -> dict:
    key = jax.random.key(seed)
    ks = iter(jax.random.split(key, 32))

    def nrm(shape, scale):
        return jax.random.normal(next(ks), shape, jnp.float32) * scale

    d = D_MODEL
    gamma0 = 1.0 - 2.0 ** (-5.0 - jnp.arange(RB_HEADS, dtype=jnp.float32))
    return {
        'x': nrm((BATCH, SEQ, d), 1.0),
        'c': nrm((BATCH, d), 1.0),
        'ctx': nrm((BATCH, CTX_LEN, d), 1.0),
        'c_ctx': nrm((d,), 1.0),
        'ada_w': nrm((DEPTH, d, 6 * d), d ** -0.5),
        'ada_b': nrm((DEPTH, 6 * d), 0.02),
        'ln_w': 1.0 + nrm((DEPTH, 2, d), 0.02),
        'ln_b': nrm((DEPTH, 2, d), 0.02),
        'even_w_in': nrm((N_EVEN, d, EVEN_IN), d ** -0.5),
        'even_w_out': nrm((N_EVEN, EVEN_OUT, d), EVEN_OUT ** -0.5 * DN_BETA),
        'ha_lb': nrm((N_EVEN, 2, HA_KEY), 0.1),
        'ha_norm': 1.0 + nrm((N_EVEN, HA_DV), 0.02),
        'rb_decay': (jnp.log(gamma0) - jnp.log1p(-gamma0)) + nrm((N_EVEN, 2, RB_HEADS), 0.01),
        'odd_w_in': nrm((N_ODD, d, ODD_IN), d ** -0.5),
        'odd_w_out': nrm((N_ODD, ODD_OUT, d), ODD_OUT ** -0.5 * DN_BETA),
        'gc_w2': nrm((N_ODD, 2, GC_RANK, GC_KEY), GC_RANK ** -0.5),
        'gc_b2': nrm((N_ODD, 2, GC_KEY), 0.1),
        'gc_norm': 1.0 + nrm((N_ODD, GC_DV), 0.02),
        'router_w': nrm((DEPTH, d, N_EXPERTS), d ** -0.5),
        'exp_w_gate': nrm((DEPTH, N_EXPERTS, d, EXPERT_FF), d ** -0.5),
        'exp_w_up': nrm((DEPTH, N_EXPERTS, d, EXPERT_FF), d ** -0.5),
        'exp_w_down': nrm((DEPTH, N_EXPERTS, EXPERT_FF, d), EXPERT_FF ** -0.5 * DN_BETA),
    }


def reference(x, c, ctx, c_ctx, ada_w, ada_b, ln_w, ln_b, even_w_in, even_w_out, ha_lb, ha_norm,
              rb_decay, odd_w_in, odd_w_out, gc_w2, gc_b2, gc_norm, router_w, exp_w_gate, exp_w_up,
              exp_w_down):
    n_lat = x.shape[1]
    rows = n_lat // GRID_W
    rope = _axial_rope(rows)
    b_ctx = ctx.shape[0]
    lb_p = jax.nn.softmax(ha_lb.astype(jnp.float32), axis=0)
    lb_all = jnp.cumsum(lb_p, axis=0) - lb_p[0]
    cond_lat = jax.nn.silu(c)
    cond_ctx = jax.nn.silu(c_ctx)

    for l in range(DEPTH):
        last = l == DEPTH - 1
        mod_lat = (cond_lat @ ada_w[l] + ada_b[l])[:, None, :]
        mod_ctx = (cond_ctx @ ada_w[l] + ada_b[l])[None, None, :]
        sh1, sc1, g1, sh2, sc2, g2 = jnp.split(mod_lat, 6, axis=-1)
        csh1, csc1, cg1, csh2, csc2, cg2 = jnp.split(mod_ctx, 6, axis=-1)
        h_lat = x * (1.0 + sc1) + sh1
        h_ctx = ctx * (1.0 + csc1) + csh1

        if l % 2 == 0:
            j = l // 2
            z_a = jnp.zeros((b_ctx, HA_HEADS, HA_DK, HA_DV), jnp.float32)
            z_b = jnp.zeros((b_ctx, RB_HEADS, RB_DK, RB_DV), jnp.float32)
            log_gamma = jax.nn.log_sigmoid(rb_decay[j].astype(jnp.float32))
            y_ctx, st = _even_mixer(h_ctx, even_w_in[j], even_w_out[j], lb_all[j], ha_norm[j],
                                    log_gamma, None, ((z_a, z_a), (z_b, z_b)))
            y_lat, _ = _even_mixer(h_lat, even_w_in[j], even_w_out[j], lb_all[j], ha_norm[j],
                                   log_gamma, rope, st)
        else:
            j = l // 2
            z_c = jnp.zeros((b_ctx, GC_HEADS, GC_DK, GC_DV), jnp.float32)
            y_ctx, st = _odd_mixer(h_ctx, odd_w_in[j], odd_w_out[j], gc_w2[j], gc_b2[j], gc_norm[j],
                                   (z_c, z_c))
            y_lat, _ = _odd_mixer(h_lat, odd_w_in[j], odd_w_out[j], gc_w2[j], gc_b2[j], gc_norm[j], st)

        ffn = (router_w[l], exp_w_gate[l], exp_w_up[l], exp_w_down[l])
        x = _layer_norm(DN_ALPHA * x + g1 * y_lat, ln_w[l, 0], ln_b[l, 0])
        x = _layer_norm(DN_ALPHA * x + g2 * _ec_ffn(x * (1.0 + sc2) + sh2, *ffn), ln_w[l, 1], ln_b[l, 1])
        if not last:
            ctx = _layer_norm(DN_ALPHA * ctx + cg1 * y_ctx, ln_w[l, 0], ln_b[l, 0])
            ctx = _layer_norm(DN_ALPHA * ctx + cg2 * _ec_ffn(ctx * (1.0 + csc2) + csh2, *ffn),
                              ln_w[l, 1], ln_b[l, 1])
    return x
```

```python
import functools

import jax
import jax.numpy as jnp
import numpy as np
from jax import lax
from jax.experimental import pallas as pl
from jax.experimental.pallas import tpu as pltpu

D_MODEL = 1024
DEPTH = 4
GRID_W = 64

HA_HEADS = 4
HA_DK = 128
HA_DV = 128
RB_HEADS = 4
RB_DK = 128
RB_DV = 128
GC_HEADS = 4
GC_DK = 128
GC_DV = 256
GC_RANK = 16
GC_TAU = 16.0
N_EXPERTS = 16
EXPERT_FF = 2816
EC_CAPACITY_FACTOR = 2

CHUNK = 64
SUB = 16
N_SUB = CHUNK // SUB
ROPE_BASE = 10000.0
LN_EPS = 1e-5
RMS_EPS = 1e-6
LB_FLOOR = 1e-30
DN_ALPHA = (2 * DEPTH) ** 0.25

HA_KEY = HA_HEADS * HA_DK
HA_VAL = HA_HEADS * HA_DV
RB_KEY = RB_HEADS * RB_DK
RB_VAL = RB_HEADS * RB_DV
GC_KEY = GC_HEADS * GC_DK
GC_VAL = GC_HEADS * GC_DV
EVEN_SIZES = (HA_KEY, HA_KEY, HA_KEY, HA_VAL, HA_VAL, RB_KEY, RB_KEY, RB_VAL, RB_VAL)
ODD_SIZES = (GC_KEY, GC_KEY, GC_VAL, GC_VAL, 2 * GC_RANK)

VMEM_LIMIT_BYTES = 56 * 1024 * 1024
FFN_ROWS_PER_STEP = 2048
FFN_FF_TILE = 256


def _split(p, sizes):
    return jnp.split(p, np.cumsum(sizes)[:-1].tolist(), axis=-1)


def _heads(t, n_heads):
    b, l, _ = t.shape
    return t.reshape(b, l, n_heads, -1).transpose(0, 2, 1, 3)


def _merge(t):
    b, h, l, d = t.shape
    return t.transpose(0, 2, 1, 3).reshape(b, l, h * d)


def _layer_norm(x, w, b):
    mu = x.mean(-1, keepdims=True)
    var = jnp.square(x - mu).mean(-1, keepdims=True)
    return (x - mu) * lax.rsqrt(var + LN_EPS) * w + b


def _rms_norm(x, w=None):
    y = x * lax.rsqrt(jnp.mean(jnp.square(x), -1, keepdims=True) + RMS_EPS)
    return y if w is None else y * w


def _axial_rope(rows):
    r_idx, c_idx = jnp.meshgrid(jnp.arange(rows), jnp.arange(GRID_W), indexing='ij')
    n_freq = RB_DK // 4
    freq = ROPE_BASE ** (-jnp.arange(n_freq, dtype=jnp.float32) / n_freq)
    ang = jnp.concatenate([r_idx.reshape(-1, 1).astype(jnp.float32) * freq,
                           c_idx.reshape(-1, 1).astype(jnp.float32) * freq], axis=-1)
    return jnp.cos(ang), jnp.sin(ang)


def _apply_rope(t, cos, sin):
    t1, t2 = jnp.split(t, 2, axis=-1)
    return jnp.concatenate([t1 * cos - t2 * sin, t1 * sin + t2 * cos], axis=-1)


def _state_pass(chunk_kv, chunk_decay, s0):
    def step(s, inp):
        kv, dec = inp
        return dec[..., None] * s + kv, s
    s_final, s_prev = lax.scan(step, s0, (jnp.moveaxis(chunk_kv, 2, 0), jnp.moveaxis(chunk_decay, 2, 0)))
    return jnp.moveaxis(s_prev, 0, 2), s_final


def _gated_chunk_scan(q, k, v, log_f, s0):
    b_, h_, l_, kd = q.shape
    vd = v.shape[-1]
    n = l_ // CHUNK
    q, k, log_f = [t.reshape(b_, h_, n, CHUNK, kd) for t in (q, k, log_f)]
    v = v.reshape(b_, h_, n, CHUNK, vd)
    cum = jnp.cumsum(log_f, axis=3)
    qs, ks, cs = [t.reshape(b_, h_, n, N_SUB, SUB, kd) for t in (q, k, cum)]
    blk_end = cs[..., -1, :]
    blk_start = jnp.concatenate([jnp.zeros_like(blk_end[:, :, :, :1]), blk_end[:, :, :, :-1]], axis=3)
    mid = cs[..., SUB // 2 - 1:SUB // 2, :]
    diag = jnp.einsum('bhnaik,bhnajk->bhnaij', qs * jnp.exp(cs - mid), ks * jnp.exp(mid - cs))
    diag = jnp.where(jnp.tril(jnp.ones((SUB, SUB), bool)), diag, 0.0)
    q_in = qs * jnp.exp(cs - blk_start[..., None, :])
    k_out = ks * jnp.exp(blk_end[..., None, :] - cs)
    cross_dec = jnp.exp(jnp.minimum(blk_start[:, :, :, :, None, :] - blk_end[:, :, :, None, :, :], 0.0))
    cross = jnp.einsum('bhnaik,bhnack,bhncjk->bhnaicj', q_in, cross_dec, k_out)
    blk_lower = jnp.arange(N_SUB)[:, None] > jnp.arange(N_SUB)[None, :]
    cross = jnp.where(blk_lower[:, None, :, None], cross, 0.0)
    scores = cross + jnp.einsum('bhnaij,ac->bhnaicj', diag, jnp.eye(N_SUB, dtype=diag.dtype))
    scores = scores.reshape(b_, h_, n, CHUNK, CHUNK)
    last = cum[:, :, :, -1:]
    chunk_kv = jnp.einsum('bhnjk,bhnjv->bhnkv', k * jnp.exp(last - cum), v)
    s_prev, s_final = _state_pass(chunk_kv, jnp.exp(last[:, :, :, 0]), s0)
    o = (jnp.einsum('bhnij,bhnjv->bhniv', scores, v)
         + jnp.einsum('bhnik,bhnkv->bhniv', q * jnp.exp(cum), s_prev))
    return o.reshape(b_, h_, l_, vd), s_final


def _retention_chunk_scan(log_gamma, q, k, v, s0):
    b_, h_, l_, kd = q.shape
    vd = v.shape[-1]
    n = l_ // CHUNK
    q, k = [t.reshape(b_, h_, n, CHUNK, kd) for t in (q, k)]
    v = v.reshape(b_, h_, n, CHUNK, vd)
    pos = jnp.arange(CHUNK, dtype=jnp.float32)
    diff = pos[:, None] - pos[None, :]
    decay = jnp.where(diff >= 0, jnp.exp(log_gamma[:, None, None] * jnp.maximum(diff, 0.0)), 0.0)
    scores = jnp.einsum('bhnik,bhnjk->bhnij', q, k) * decay[None, :, None]
    k_dec = jnp.exp(log_gamma[:, None] * (CHUNK - 1 - pos))[None, :, None, :, None]
    q_dec = jnp.exp(log_gamma[:, None] * (pos + 1))[None, :, None, :, None]
    chunk_kv = jnp.einsum('bhnjk,bhnjv->bhnkv', k * k_dec, v)
    chunk_decay = jnp.broadcast_to(jnp.exp(log_gamma * CHUNK)[None, :, None, None], (b_, h_, n, kd))
    s_prev, s_final = _state_pass(chunk_kv, chunk_decay, s0)
    o = (jnp.einsum('bhnij,bhnjv->bhniv', scores, v)
         + jnp.einsum('bhnik,bhnkv->bhniv', q * q_dec, s_prev))
    return o.reshape(b_, h_, l_, vd), s_final


def _bidirectional(scan_f, scan_b, args_f, args_b, s0):
    o_f, s_f = scan_f(*args_f, s0[0])
    o_b, s_b = scan_b(*[jnp.flip(a, axis=2) for a in args_b], s0[1])
    return o_f + jnp.flip(o_b, axis=2), (s_f, s_b)


def _hgrn2_forget(z, lb):
    log_f = jnp.logaddexp(z, jnp.log(jnp.maximum(lb, LB_FLOOR))) - jax.nn.softplus(z)
    key = (1.0 - lb) * jax.nn.sigmoid(-z)
    return log_f, key


def _even_mixer(h, w_in, w_out, lb, ha_norm_w, rb_log_gamma, rope, states):
    p = jnp.einsum('bld,de->ble', h, w_in)
    qa, fa_f, fa_b, ia, ga, qb, kb, vb, gb = _split(p, EVEN_SIZES)
    qa = _heads(qa, HA_HEADS) * HA_DK ** -0.5
    ia = _heads(ia, HA_HEADS)
    lb = lb.reshape(2, HA_HEADS, 1, HA_DK)
    lf_f, ka_f = _hgrn2_forget(_heads(fa_f, HA_HEADS), lb[0])
    lf_b, ka_b = _hgrn2_forget(_heads(fa_b, HA_HEADS), lb[1])
    oa, st_a = _bidirectional(_gated_chunk_scan, _gated_chunk_scan,
                              (qa, ka_f, ia, lf_f), (qa, ka_b, ia, lf_b), states[0])
    ya = _merge(_rms_norm(oa, ha_norm_w)) * jax.nn.silu(ga)
    qb = _heads(qb, RB_HEADS) * RB_DK ** -0.5
    kb = _heads(kb, RB_HEADS)
    if rope is not None:
        qb = _apply_rope(qb, *rope)
        kb = _apply_rope(kb, *rope)
    vb = _heads(vb, RB_HEADS)
    ob, st_b = _bidirectional(functools.partial(_retention_chunk_scan, rb_log_gamma[0]),
                              functools.partial(_retention_chunk_scan, rb_log_gamma[1]),
                              (qb, kb, vb), (qb, kb, vb), states[1])
    yb = _merge(_rms_norm(ob)) * jax.nn.silu(gb)
    y = jnp.concatenate([ya, yb], axis=-1)
    return jnp.einsum('ble,ed->bld', y, w_out), (st_a, st_b)


def _odd_mixer(h, w_in, w_out, w2, b2, norm_w, states):
    p = jnp.einsum('bld,de->ble', h, w_in)
    qc, kc, vc, gc, lr = _split(p, ODD_SIZES)
    q = _heads(qc, GC_HEADS) * GC_DK ** -0.5
    k = _heads(kc, GC_HEADS)
    v = _heads(vc, GC_HEADS)
    lr_f, lr_b = jnp.split(lr, 2, axis=-1)
    lf_f = _heads(jax.nn.log_sigmoid(lr_f @ w2[0] + b2[0]) / GC_TAU, GC_HEADS)
    lf_b = _heads(jax.nn.log_sigmoid(lr_b @ w2[1] + b2[1]) / GC_TAU, GC_HEADS)
    o, st = _bidirectional(_gated_chunk_scan, _gated_chunk_scan,
                           (q, k, v, lf_f), (q, k, v, lf_b), states)
    y = _merge(_rms_norm(o, norm_w)) * jax.nn.silu(gc)
    return jnp.einsum('ble,ed->bld', y, w_out), st


def _expert_ffn_body(x_ref, gate_ref, wg_ref, wu_ref, wd_ref, out_ref):
    j = pl.program_id(2)
    tb, cap, d = x_ref.shape
    x = x_ref[...].reshape(tb * cap, d)
    g = jnp.dot(x, wg_ref[...], preferred_element_type=jnp.float32)
    u = jnp.dot(x, wu_ref[...], preferred_element_type=jnp.float32)
    hid = (g * jax.nn.sigmoid(g) * u).astype(jnp.bfloat16)
    y = jnp.dot(hid, wd_ref[...], preferred_element_type=jnp.float32).reshape(tb, cap, d)

    @pl.when(j == 0)
    def _():
        out_ref[...] = y

    @pl.when(j > 0)
    def _():
        out_ref[...] += y

    @pl.when(j == pl.num_programs(2) - 1)
    def _():
        out_ref[...] = out_ref[...] * gate_ref[...]


def _expert_ffn(xs, gate, w_gate, w_up, w_down):
    b, e, cap, d = xs.shape
    ff = w_gate.shape[-1]
    tb = max(1, min(b, FFN_ROWS_PER_STEP // cap))
    tf = FFN_FF_TILE
    assert b % tb == 0 and ff % tf == 0
    return pl.pallas_call(
        _expert_ffn_body,
        grid=(e, b // tb, ff // tf),
        in_specs=[
            pl.BlockSpec((tb, None, cap, d), lambda ei, bi, j: (bi, ei, 0, 0)),
            pl.BlockSpec((tb, None, cap, 1), lambda ei, bi, j: (bi, ei, 0, 0)),
            pl.BlockSpec((None, d, tf), lambda ei, bi, j: (ei, 0, j)),
            pl.BlockSpec((None, d, tf), lambda ei, bi, j: (ei, 0, j)),
            pl.BlockSpec((None, tf, d), lambda ei, bi, j: (ei, j, 0)),
        ],
        out_specs=pl.BlockSpec((tb, None, cap, d), lambda ei, bi, j: (bi, ei, 0, 0)),
        out_shape=jax.ShapeDtypeStruct((b, e, cap, d), jnp.float32),
        compiler_params=pltpu.CompilerParams(
            dimension_semantics=("parallel", "parallel", "arbitrary"),
            vmem_limit_bytes=VMEM_LIMIT_BYTES),
        name="expert_ffn",
    )(xs, gate, w_gate, w_up, w_down)


def _ec_ffn(h, w_router, w_gate, w_up, w_down):
    b, n, d = h.shape
    cap = EC_CAPACITY_FACTOR * n // N_EXPERTS
    aff = jax.nn.softmax(jnp.einsum('bnd,de->bne', h, w_router), axis=-1)
    gate, idx = lax.top_k(jnp.swapaxes(aff, 1, 2), cap)
    xs = jax.vmap(lambda hb, ib: hb[ib])(h.astype(jnp.bfloat16), idx)
    ys = _expert_ffn(xs, gate[..., None], w_gate, w_up, w_down)
    return jax.vmap(lambda yb, ib: jnp.zeros((n, d), h.dtype).at[ib.reshape(-1)].add(yb.reshape(-1, d)))(ys, idx)


def kernel(x, c, ctx, c_ctx, ada_w, ada_b, ln_w, ln_b, even_w_in, even_w_out, ha_lb, ha_norm,
           rb_decay, odd_w_in, odd_w_out, gc_w2, gc_b2, gc_norm, router_w, exp_w_gate, exp_w_up,
           exp_w_down):
    n_lat = x.shape[1]
    rows = n_lat // GRID_W
    rope = _axial_rope(rows)
    b_ctx = ctx.shape[0]
    lb_p = jax.nn.softmax(ha_lb, axis=0)
    lb_all = jnp.cumsum(lb_p, axis=0) - lb_p[0]
    cond_lat = jax.nn.silu(c)
    cond_ctx = jax.nn.silu(c_ctx)

    for l in range(DEPTH):
        last = l == DEPTH - 1
        mod_lat = (cond_lat @ ada_w[l] + ada_b[l])[:, None, :]
        mod_ctx = (cond_ctx @ ada_w[l] + ada_b[l])[None, None, :]
        sh1, sc1, g1, sh2, sc2, g2 = jnp.split(mod_lat, 6, axis=-1)
        csh1, csc1, cg1, csh2, csc2, cg2 = jnp.split(mod_ctx, 6, axis=-1)
        h_lat = x * (1.0 + sc1) + sh1
        h_ctx = ctx * (1.0 + csc1) + csh1

        j = l // 2
        if l % 2 == 0:
            z_a = jnp.zeros((b_ctx, HA_HEADS, HA_DK, HA_DV), jnp.float32)
            z_b = jnp.zeros((b_ctx, RB_HEADS, RB_DK, RB_DV), jnp.float32)
            log_gamma = jax.nn.log_sigmoid(rb_decay[j])
            y_ctx, st = _even_mixer(h_ctx, even_w_in[j], even_w_out[j], lb_all[j], ha_norm[j],
                                    log_gamma, None, ((z_a, z_a), (z_b, z_b)))
            y_lat, _ = _even_mixer(h_lat, even_w_in[j], even_w_out[j], lb_all[j], ha_norm[j],
                                   log_gamma, rope, st)
        else:
            z_c = jnp.zeros((b_ctx, GC_HEADS, GC_DK, GC_DV), jnp.float32)
            y_ctx, st = _odd_mixer(h_ctx, odd_w_in[j], odd_w_out[j], gc_w2[j], gc_b2[j], gc_norm[j],
                                   (z_c, z_c))
            y_lat, _ = _odd_mixer(h_lat, odd_w_in[j], odd_w_out[j], gc_w2[j], gc_b2[j], gc_norm[j], st)

        ffn = (router_w[l], exp_w_gate[l].astype(jnp.bfloat16), exp_w_up[l].astype(jnp.bfloat16),
               exp_w_down[l].astype(jnp.bfloat16))
        x = _layer_norm(DN_ALPHA * x + g1 * y_lat, ln_w[l, 0], ln_b[l, 0])
        x = _layer_norm(DN_ALPHA * x + g2 * _ec_ffn(x * (1.0 + sc2) + sh2, *ffn), ln_w[l, 1], ln_b[l, 1])
        if not last:
            ctx = _layer_norm(DN_ALPHA * ctx + cg1 * y_ctx, ln_w[l, 0], ln_b[l, 0])
            ctx = _layer_norm(DN_ALPHA * ctx + cg2 * _ec_ffn(ctx * (1.0 + csc2) + csh2, *ffn),
                              ln_w[l, 1], ln_b[l, 1])
    return x
```

```python
import functools

import jax
import jax.numpy as jnp
import numpy as np
from jax import lax
from jax.experimental import pallas as pl
from jax.experimental.pallas import tpu as pltpu

D_MODEL = 1024
DEPTH = 4
GRID_W = 64

HA_HEADS = 4
HA_DK = 128
HA_DV = 128
RB_HEADS = 4
RB_DK = 128
RB_DV = 128
GC_HEADS = 4
GC_DK = 128
GC_DV = 256
GC_RANK = 16
GC_TAU = 16.0
N_EXPERTS = 16
EXPERT_FF = 2816
EC_CAPACITY_FACTOR = 2

CHUNK = 64
SUB = 16
N_SUB = CHUNK // SUB
ROPE_BASE = 10000.0
LN_EPS = 1e-5
RMS_EPS = 1e-6
LB_FLOOR = 1e-30
DN_ALPHA = (2 * DEPTH) ** 0.25

HA_KEY = HA_HEADS * HA_DK
HA_VAL = HA_HEADS * HA_DV
RB_KEY = RB_HEADS * RB_DK
RB_VAL = RB_HEADS * RB_DV
GC_KEY = GC_HEADS * GC_DK
GC_VAL = GC_HEADS * GC_DV
EVEN_SIZES = (HA_KEY, HA_KEY, HA_KEY, HA_VAL, HA_VAL, RB_KEY, RB_KEY, RB_VAL, RB_VAL)
ODD_SIZES = (GC_KEY, GC_KEY, GC_VAL, GC_VAL, 2 * GC_RANK)

VMEM_LIMIT_BYTES = 56 * 1024 * 1024
FFN_ROWS_PER_STEP = 2048
FFN_FF_TILE = 256

_NT = (((1,), (1,)), ((), ()))
_TN = (((0,), (0,)), ((), ()))


def _split(p, sizes):
    return jnp.split(p, np.cumsum(sizes)[:-1].tolist(), axis=-1)


def _layer_norm(x, w, b):
    mu = x.mean(-1, keepdims=True)
    var = jnp.square(x - mu).mean(-1, keepdims=True)
    return (x - mu) * lax.rsqrt(var + LN_EPS) * w + b


def _head_rms_norm(o, n_heads, w=None):
    b, l, e = o.shape
    o = o.reshape(b, l, n_heads, e // n_heads)
    y = o * lax.rsqrt(jnp.mean(jnp.square(o), -1, keepdims=True) + RMS_EPS)
    if w is not None:
        y = y * w
    return y.reshape(b, l, e)


def _axial_rope(rows):
    r_idx, c_idx = jnp.meshgrid(jnp.arange(rows), jnp.arange(GRID_W), indexing='ij')
    n_freq = RB_DK // 4
    freq = ROPE_BASE ** (-jnp.arange(n_freq, dtype=jnp.float32) / n_freq)
    ang = jnp.concatenate([r_idx.reshape(-1, 1).astype(jnp.float32) * freq,
                           c_idx.reshape(-1, 1).astype(jnp.float32) * freq], axis=-1)
    return jnp.cos(ang), jnp.sin(ang)


def _apply_rope(t, n_heads, cos, sin):
    b, l, e = t.shape
    t = t.reshape(b, l, n_heads, e // n_heads)
    t1, t2 = jnp.split(t, 2, axis=-1)
    cos, sin = cos[:, None, :], sin[:, None, :]
    return jnp.concatenate([t1 * cos - t2 * sin, t1 * sin + t2 * cos], axis=-1).reshape(b, l, e)


def _scan_chunk(q, k, v, lf, st, tri, dmask, rev):
    bf = jnp.bfloat16
    dk = q.shape[-1]
    cs = jnp.dot(tri, lf, precision=lax.Precision.HIGHEST, preferred_element_type=jnp.float32)
    zero_row = jnp.zeros((1, dk), jnp.float32)
    if not rev:
        ends = [cs[SUB * a + SUB - 1:SUB * a + SUB, :] for a in range(N_SUB)]
        starts = [zero_row] + ends[:-1]
        mids = [cs[SUB * a + SUB // 2 - 1:SUB * a + SUB // 2, :] for a in range(N_SUB)]
        order = list(range(N_SUB))
    else:
        ends = [cs[SUB * a:SUB * a + 1, :] for a in range(N_SUB)]
        starts = ends[1:] + [zero_row]
        mids = [cs[SUB * a + SUB // 2:SUB * a + SUB // 2 + 1, :] for a in range(N_SUB)]
        order = list(range(N_SUB - 1, -1, -1))
    last = ends[order[-1]]

    def per_block(rows):
        return jnp.concatenate([jnp.broadcast_to(r, (SUB, dk)) for r in rows], axis=0)

    mid_b, start_b, end_b = per_block(mids), per_block(starts), per_block(ends)
    qd = q * jnp.exp(cs - mid_b)
    kd = k * jnp.exp(mid_b - cs)
    q_in = q * jnp.exp(cs - start_b)
    k_out = k * jnp.exp(end_b - cs)
    q_state = q_in * jnp.exp(start_b)
    k_state = k_out * jnp.exp(last - end_b)

    def blk(t, a):
        return t[SUB * a:SUB * (a + 1), :]

    zeros_blk = jnp.zeros((SUB, dk), jnp.float32)
    q_cat, k_cat = [], []
    for s in range(N_SUB - 1):
        src = order[s]
        q_rows = [zeros_blk] * N_SUB
        k_rows = [zeros_blk] * N_SUB
        k_rows[src] = blk(k_out, src)
        for t in range(s + 1, N_SUB):
            tgt = order[t]
            piece = blk(q_in, tgt)
            if t > s + 1:
                piece = piece * jnp.exp(jnp.minimum(starts[tgt] - ends[src], 0.0))
            q_rows[tgt] = piece
        q_cat.append(jnp.concatenate(q_rows, axis=0))
        k_cat.append(jnp.concatenate(k_rows, axis=0))
    q_cat = jnp.concatenate(q_cat, axis=1).astype(bf)
    k_cat = jnp.concatenate(k_cat, axis=1).astype(bf)
    cross = lax.dot_general(q_cat, k_cat, _NT, preferred_element_type=jnp.float32)
    diag = lax.dot_general(qd.astype(bf), kd.astype(bf), _NT, preferred_element_type=jnp.float32)
    scores = jnp.where(dmask, diag, 0.0) + cross
    o = (jnp.dot(scores.astype(bf), v, preferred_element_type=jnp.float32)
         + lax.dot_general(q_state.astype(bf), st.astype(bf), _NT, preferred_element_type=jnp.float32))
    st_new = st * jnp.exp(last) + lax.dot_general(v, k_state.astype(bf), _TN, preferred_element_type=jnp.float32)
    return o, st_new


def _bidir_scan_body(*refs, n_chunks, scalar_decay):
    if scalar_decay:
        q_ref, kf_ref, kb_ref, v_ref, lg_ref, s0f_ref, s0b_ref, o_ref, sf_ref, sb_ref = refs
    else:
        q_ref, kf_ref, kb_ref, v_ref, lff_ref, lfb_ref, s0f_ref, s0b_ref, o_ref, sf_ref, sb_ref = refs
    c = CHUNK
    dk = q_ref.shape[-1]
    row = lax.broadcasted_iota(jnp.int32, (c, c), 0)
    col = lax.broadcasted_iota(jnp.int32, (c, c), 1)
    same_blk = (row // SUB) == (col // SUB)
    tri_f = (col <= row).astype(jnp.float32)
    tri_b = (col >= row).astype(jnp.float32)
    dmask_f = same_blk & (col <= row)
    dmask_b = same_blk & (col >= row)
    sf_ref[...] = s0f_ref[...]
    sb_ref[...] = s0b_ref[...]

    def step(n, accumulate):
        rf = pl.ds(pl.multiple_of(n * c, c), c)
        rb = pl.ds(pl.multiple_of((n_chunks - 1 - n) * c, c), c)
        if scalar_decay:
            lf_f = jnp.broadcast_to(lg_ref[0, 0:1, :], (c, dk))
            lf_b = jnp.broadcast_to(lg_ref[1, 0:1, :], (c, dk))
        else:
            lf_f = lff_ref[rf, :]
            lf_b = lfb_ref[rb, :]
        o_f, st_f = _scan_chunk(q_ref[rf, :], kf_ref[rf, :], v_ref[rf, :].astype(jnp.bfloat16), lf_f,
                                sf_ref[...], tri_f, dmask_f, False)
        o_b, st_b = _scan_chunk(q_ref[rb, :], kb_ref[rb, :], v_ref[rb, :].astype(jnp.bfloat16), lf_b,
                                sb_ref[...], tri_b, dmask_b, True)
        sf_ref[...] = st_f
        sb_ref[...] = st_b
        if accumulate:
            o_ref[rf, :] += o_f
            o_ref[rb, :] += o_b
        else:
            o_ref[rf, :] = o_f
            o_ref[rb, :] = o_b

    half = n_chunks // 2

    def first(n, carry):
        step(n, False)
        return carry

    def second(n, carry):
        step(n, True)
        return carry

    lax.fori_loop(0, half, first, 0)
    lax.fori_loop(half, n_chunks, second, 0)


def _bidir_scan(q, k_f, k_b, v, lf_f, lf_b, s0_f, s0_b, n_heads, log_decay=None):
    b, l, hk = q.shape
    dk = hk // n_heads
    dv = v.shape[-1] // n_heads
    assert l % (2 * CHUNK) == 0
    scalar_decay = log_decay is not None
    seq_k = pl.BlockSpec((None, l, dk), lambda bi, hi: (bi, 0, hi))
    seq_v = pl.BlockSpec((None, l, dv), lambda bi, hi: (bi, 0, hi))
    state = pl.BlockSpec((None, None, dv, dk), lambda bi, hi: (bi, hi, 0, 0))
    if scalar_decay:
        lg = jnp.broadcast_to(log_decay.astype(jnp.float32)[:, :, None, None], (2, n_heads, 8, dk))
        decay_args = (lg,)
        decay_specs = [pl.BlockSpec((2, None, 8, dk), lambda bi, hi: (0, hi, 0, 0))]
    else:
        decay_args = (lf_f, lf_b)
        decay_specs = [seq_k, seq_k]
    return pl.pallas_call(
        functools.partial(_bidir_scan_body, n_chunks=l // CHUNK, scalar_decay=scalar_decay),
        grid=(b, n_heads),
        in_specs=[seq_k, seq_k, seq_k, seq_v] + decay_specs + [state, state],
        out_specs=[seq_v, state, state],
        out_shape=[jax.ShapeDtypeStruct((b, l, n_heads * dv), jnp.float32),
                   jax.ShapeDtypeStruct((b, n_heads, dv, dk), jnp.float32),
                   jax.ShapeDtypeStruct((b, n_heads, dv, dk), jnp.float32)],
        compiler_params=pltpu.CompilerParams(
            dimension_semantics=("parallel", "parallel"),
            vmem_limit_bytes=VMEM_LIMIT_BYTES),
        name="bidir_scan",
    )(q, k_f, k_b, v, *decay_args, s0_f, s0_b)


def _hgrn2_forget(z, lb):
    log_f = jnp.logaddexp(z, jnp.log(jnp.maximum(lb, LB_FLOOR))) - jax.nn.softplus(z)
    key = (1.0 - lb) * jax.nn.sigmoid(-z)
    return log_f, key


def _even_mixer(h, w_in, w_out, lb, ha_norm_w, rb_log_gamma, rope, states):
    p = jnp.einsum('bld,de->ble', h, w_in)
    qa, fa_f, fa_b, ia, ga, qb, kb, vb, gb = _split(p, EVEN_SIZES)
    qa = qa * HA_DK ** -0.5
    lf_f, ka_f = _hgrn2_forget(fa_f, lb[0])
    lf_b, ka_b = _hgrn2_forget(fa_b, lb[1])
    oa, sa_f, sa_b = _bidir_scan(qa, ka_f, ka_b, ia, lf_f, lf_b, states[0][0], states[0][1], HA_HEADS)
    ya = _head_rms_norm(oa, HA_HEADS, ha_norm_w) * jax.nn.silu(ga)
    qb = qb * RB_DK ** -0.5
    if rope is not None:
        qb = _apply_rope(qb, RB_HEADS, *rope)
        kb = _apply_rope(kb, RB_HEADS, *rope)
    ob, sb_f, sb_b = _bidir_scan(qb, kb, kb, vb, None, None, states[1][0], states[1][1], RB_HEADS,
                                 log_decay=rb_log_gamma)
    yb = _head_rms_norm(ob, RB_HEADS) * jax.nn.silu(gb)
    y = jnp.concatenate([ya, yb], axis=-1)
    return jnp.einsum('ble,ed->bld', y, w_out), ((sa_f, sa_b), (sb_f, sb_b))


def _odd_mixer(h, w_in, w_out, w2, b2, norm_w, states):
    p = jnp.einsum('bld,de->ble', h, w_in)
    qc, kc, vc, gc, lr = _split(p, ODD_SIZES)
    q = qc * GC_DK ** -0.5
    lr_f, lr_b = jnp.split(lr, 2, axis=-1)
    lf_f = jax.nn.log_sigmoid(lr_f @ w2[0] + b2[0]) / GC_TAU
    lf_b = jax.nn.log_sigmoid(lr_b @ w2[1] + b2[1]) / GC_TAU
    o, s_f, s_b = _bidir_scan(q, kc, kc, vc, lf_f, lf_b, states[0], states[1], GC_HEADS)
    y = _head_rms_norm(o, GC_HEADS, norm_w) * jax.nn.silu(gc)
    return jnp.einsum('ble,ed->bld', y, w_out), (s_f, s_b)


def _expert_ffn_body(x_ref, gate_ref, wg_ref, wu_ref, wd_ref, out_ref):
    j = pl.program_id(2)
    tb, cap, d = x_ref.shape
    x = x_ref[...].reshape(tb * cap, d)
    g = jnp.dot(x, wg_ref[...], preferred_element_type=jnp.float32)
    u = jnp.dot(x, wu_ref[...], preferred_element_type=jnp.float32)
    hid = (g * jax.nn.sigmoid(g) * u).astype(jnp.bfloat16)
    y = jnp.dot(hid, wd_ref[...], preferred_element_type=jnp.float32).reshape(tb, cap, d)

    @pl.when(j == 0)
    def _():
        out_ref[...] = y

    @pl.when(j > 0)
    def _():
        out_ref[...] += y

    @pl.when(j == pl.num_programs(2) - 1)
    def _():
        out_ref[...] = out_ref[...] * gate_ref[...]


def _expert_ffn(xs, gate, w_gate, w_up, w_down):
    b, e, cap, d = xs.shape
    ff = w_gate.shape[-1]
    tb = max(1, min(b, FFN_ROWS_PER_STEP // cap))
    tf = FFN_FF_TILE
    assert b % tb == 0 and ff % tf == 0
    return pl.pallas_call(
        _expert_ffn_body,
        grid=(e, b // tb, ff // tf),
        in_specs=[
            pl.BlockSpec((tb, None, cap, d), lambda ei, bi, j: (bi, ei, 0, 0)),
            pl.BlockSpec((tb, None, cap, 1), lambda ei, bi, j: (bi, ei, 0, 0)),
            pl.BlockSpec((None, d, tf), lambda ei, bi, j: (ei, 0, j)),
            pl.BlockSpec((None, d, tf), lambda ei, bi, j: (ei, 0, j)),
            pl.BlockSpec((None, tf, d), lambda ei, bi, j: (ei, j, 0)),
        ],
        out_specs=pl.BlockSpec((tb, None, cap, d), lambda ei, bi, j: (bi, ei, 0, 0)),
        out_shape=jax.ShapeDtypeStruct((b, e, cap, d), jnp.float32),
        compiler_params=pltpu.CompilerParams(
            dimension_semantics=("parallel", "parallel", "arbitrary"),
            vmem_limit_bytes=VMEM_LIMIT_BYTES),
        name="expert_ffn",
    )(xs, gate, w_gate, w_up, w_down)


def _ec_ffn(h, w_router, w_gate, w_up, w_down):
    b, n, d = h.shape
    cap = EC_CAPACITY_FACTOR * n // N_EXPERTS
    aff = jax.nn.softmax(jnp.einsum('bnd,de->bne', h, w_router), axis=-1)
    gate, idx = lax.top_k(jnp.swapaxes(aff, 1, 2), cap)
    xs = jax.vmap(lambda hb, ib: hb[ib])(h.astype(jnp.bfloat16), idx)
    ys = _expert_ffn(xs, gate[..., None], w_gate, w_up, w_down)
    return jax.vmap(lambda yb, ib: jnp.zeros((n, d), h.dtype).at[ib.reshape(-1)].add(yb.reshape(-1, d)))(ys, idx)


def kernel(x, c, ctx, c_ctx, ada_w, ada_b, ln_w, ln_b, even_w_in, even_w_out, ha_lb, ha_norm,
           rb_decay, odd_w_in, odd_w_out, gc_w2, gc_b2, gc_norm, router_w, exp_w_gate, exp_w_up,
           exp_w_down):
    n_lat = x.shape[1]
    rows = n_lat // GRID_W
    rope = _axial_rope(rows)
    b_ctx = ctx.shape[0]
    lb_p = jax.nn.softmax(ha_lb, axis=0)
    lb_all = jnp.cumsum(lb_p, axis=0) - lb_p[0]
    cond_lat = jax.nn.silu(c)
    cond_ctx = jax.nn.silu(c_ctx)

    for l in range(DEPTH):
        last = l == DEPTH - 1
        mod_lat = (cond_lat @ ada_w[l] + ada_b[l])[:, None, :]
        mod_ctx = (cond_ctx @ ada_w[l] + ada_b[l])[None, None, :]
        sh1, sc1, g1, sh2, sc2, g2 = jnp.split(mod_lat, 6, axis=-1)
        csh1, csc1, cg1, csh2, csc2, cg2 = jnp.split(mod_ctx, 6, axis=-1)
        h_lat = x * (1.0 + sc1) + sh1
        h_ctx = ctx * (1.0 + csc1) + csh1

        j = l // 2
        if l % 2 == 0:
            z_a = jnp.zeros((b_ctx, HA_HEADS, HA_DV, HA_DK), jnp.float32)
            z_b = jnp.zeros((b_ctx, RB_HEADS, RB_DV, RB_DK), jnp.float32)
            log_gamma = jax.nn.log_sigmoid(rb_decay[j])
            y_ctx, st = _even_mixer(h_ctx, even_w_in[j], even_w_out[j], lb_all[j], ha_norm[j],
                                    log_gamma, None, ((z_a, z_a), (z_b, z_b)))
            y_lat, _ = _even_mixer(h_lat, even_w_in[j], even_w_out[j], lb_all[j], ha_norm[j],
                                   log_gamma, rope, st)
        else:
            z_c = jnp.zeros((b_ctx, GC_HEADS, GC_DV, GC_DK), jnp.float32)
            y_ctx, st = _odd_mixer(h_ctx, odd_w_in[j], odd_w_out[j], gc_w2[j], gc_b2[j], gc_norm[j],
                                   (z_c, z_c))
            y_lat, _ = _odd_mixer(h_lat, odd_w_in[j], odd_w_out[j], gc_w2[j], gc_b2[j], gc_norm[j], st)

        ffn = (router_w[l], exp_w_gate[l].astype(jnp.bfloat16), exp_w_up[l].astype(jnp.bfloat16),
               exp_w_down[l].astype(jnp.bfloat16))
        x = _layer_norm(DN_ALPHA * x + g1 * y_lat, ln_w[l, 0], ln_b[l, 0])
        x = _layer_norm(DN_ALPHA * x + g2 * _ec_ffn(x * (1.0 + sc2) + sh2, *ffn), ln_w[l, 1], ln_b[l, 1])
        if not last:
            ctx = _layer_norm(DN_ALPHA * ctx + cg1 * y_ctx, ln_w[l, 0], ln_b[l, 0])
            ctx = _layer_norm(DN_ALPHA * ctx + cg2 * _ec_ffn(ctx * (1.0 + csc2) + csh2, *ffn),
                              ln_w[l, 1], ln_b[l, 1])
    return x
```

```python
import functools

import jax
import jax.numpy as jnp
import numpy as np
from jax import lax
from jax.experimental import pallas as pl
from jax.experimental.pallas import tpu as pltpu

D_MODEL = 1024
DEPTH = 4
GRID_W = 64

HA_HEADS = 4
HA_DK = 128
HA_DV = 128
RB_HEADS = 4
RB_DK = 128
RB_DV = 128
GC_HEADS = 4
GC_DK = 128
GC_DV = 256
GC_RANK = 16
GC_TAU = 16.0
N_EXPERTS = 16
EXPERT_FF = 2816
EC_CAPACITY_FACTOR = 2

CHUNK = 64
SUB = 16
N_SUB = CHUNK // SUB
ROPE_BASE = 10000.0
LN_EPS = 1e-5
RMS_EPS = 1e-6
LB_FLOOR = 1e-30
DN_ALPHA = (2 * DEPTH) ** 0.25

HA_KEY = HA_HEADS * HA_DK
HA_VAL = HA_HEADS * HA_DV
RB_KEY = RB_HEADS * RB_DK
RB_VAL = RB_HEADS * RB_DV
GC_KEY = GC_HEADS * GC_DK
GC_VAL = GC_HEADS * GC_DV
EVEN_SIZES = (HA_KEY, HA_KEY, HA_KEY, HA_VAL, HA_VAL, RB_KEY, RB_KEY, RB_VAL, RB_VAL)
ODD_SIZES = (GC_KEY, GC_KEY, GC_VAL, GC_VAL, 2 * GC_RANK)

VMEM_LIMIT_BYTES = 56 * 1024 * 1024
FFN_ROWS_PER_STEP = 1024
FFN_FF_TILE = 256
SCAN_EPILOGUE_ROWS = 256
COMBINE_ROWS = 256
SCAN_GROUP = 8

_NT = (((1,), (1,)), ((), ()))
_TN = (((0,), (0,)), ((), ()))


def _split(p, sizes):
    return jnp.split(p, np.cumsum(sizes)[:-1].tolist(), axis=-1)


def _layer_norm(x, w, b):
    mu = x.mean(-1, keepdims=True)
    var = jnp.square(x - mu).mean(-1, keepdims=True)
    return (x - mu) * lax.rsqrt(var + LN_EPS) * w + b


def _head_rms_norm(o, n_heads, w=None):
    b, l, e = o.shape
    o = o.reshape(b, l, n_heads, e // n_heads)
    y = o * lax.rsqrt(jnp.mean(jnp.square(o), -1, keepdims=True) + RMS_EPS)
    if w is not None:
        y = y * w
    return y.reshape(b, l, e)


def _axial_rope(rows):
    r_idx, c_idx = jnp.meshgrid(jnp.arange(rows), jnp.arange(GRID_W), indexing='ij')
    n_freq = RB_DK // 4
    freq = ROPE_BASE ** (-jnp.arange(n_freq, dtype=jnp.float32) / n_freq)
    ang = jnp.concatenate([r_idx.reshape(-1, 1).astype(jnp.float32) * freq,
                           c_idx.reshape(-1, 1).astype(jnp.float32) * freq], axis=-1)
    return jnp.cos(ang), jnp.sin(ang)


def _apply_rope(t, n_heads, cos, sin):
    b, l, e = t.shape
    t = t.reshape(b, l, n_heads, e // n_heads)
    t1, t2 = jnp.split(t, 2, axis=-1)
    cos, sin = cos[:, None, :], sin[:, None, :]
    return jnp.concatenate([t1 * cos - t2 * sin, t1 * sin + t2 * cos], axis=-1).reshape(b, l, e)


def _chunk_running_sum(lf, tri3):
    bf = jnp.bfloat16
    hi = lf.astype(bf)
    r1 = lf - hi.astype(jnp.float32)
    mid = r1.astype(bf)
    lo = (r1 - mid.astype(jnp.float32)).astype(bf)
    return jnp.dot(tri3, jnp.concatenate([hi, mid, lo], axis=0), preferred_element_type=jnp.float32)


def _chunk_operands(q, k, cs, rev):
    bf = jnp.bfloat16
    dk = q.shape[-1]
    zero_row = jnp.zeros((1, dk), jnp.float32)
    if not rev:
        ends = [cs[SUB * a + SUB - 1:SUB * a + SUB, :] for a in range(N_SUB)]
        starts = [zero_row] + ends[:-1]
        mids = [cs[SUB * a + SUB // 2 - 1:SUB * a + SUB // 2, :] for a in range(N_SUB)]
        order = list(range(N_SUB))
    else:
        ends = [cs[SUB * a:SUB * a + 1, :] for a in range(N_SUB)]
        starts = ends[1:] + [zero_row]
        mids = [cs[SUB * a + SUB // 2:SUB * a + SUB // 2 + 1, :] for a in range(N_SUB)]
        order = list(range(N_SUB - 1, -1, -1))
    last = ends[order[-1]]

    def per_block(rows):
        return jnp.concatenate([jnp.broadcast_to(r, (SUB, dk)) for r in rows], axis=0)

    mid_b, start_b, end_b = per_block(mids), per_block(starts), per_block(ends)
    qd = q * jnp.exp(cs - mid_b)
    kd = k * jnp.exp(mid_b - cs)
    q_in = q * jnp.exp(cs - start_b)
    k_out = k * jnp.exp(end_b - cs)
    q_state = q_in * jnp.exp(start_b)
    k_state = k_out * jnp.exp(last - end_b)

    def blk(t, a):
        return t[SUB * a:SUB * (a + 1), :]

    zeros_blk = jnp.zeros((SUB, dk), jnp.float32)
    q_cat, k_cat = [], []
    for s in range(N_SUB - 1):
        src = order[s]
        q_rows = [zeros_blk] * N_SUB
        k_rows = [zeros_blk] * N_SUB
        k_rows[src] = blk(k_out, src)
        for t in range(s + 1, N_SUB):
            tgt = order[t]
            piece = blk(q_in, tgt)
            if t > s + 1:
                piece = piece * jnp.exp(jnp.minimum(starts[tgt] - ends[src], 0.0))
            q_rows[tgt] = piece
        q_cat.append(jnp.concatenate(q_rows, axis=0))
        k_cat.append(jnp.concatenate(k_rows, axis=0))
    return dict(qd=qd.astype(bf), kd=kd.astype(bf),
                q_cat=jnp.concatenate(q_cat, axis=1).astype(bf), k_cat=jnp.concatenate(k_cat, axis=1).astype(bf),
                q_state=q_state.astype(bf), k_state=k_state.astype(bf), decay=jnp.exp(last))


def _bidir_scan_body(*refs, n_chunks, group, scalar_decay):
    if scalar_decay:
        q_ref, kf_ref, kb_ref, v_ref, lg_ref, gate_ref, nw_ref, s0f_ref, s0b_ref, o_ref, sf_ref, sb_ref = refs
    else:
        (q_ref, kf_ref, kb_ref, v_ref, lff_ref, lfb_ref, gate_ref, nw_ref, s0f_ref, s0b_ref,
         o_ref, sf_ref, sb_ref) = refs
    c = CHUNK
    bf = jnp.bfloat16
    dk = q_ref.shape[-1]
    row = lax.broadcasted_iota(jnp.int32, (c, c), 0)
    col = lax.broadcasted_iota(jnp.int32, (c, c), 1)
    same_blk = (row // SUB) == (col // SUB)
    tri = {False: (col <= row), True: (col >= row)}
    tri3 = {r: jnp.concatenate([t.astype(bf)] * 3, axis=1) for r, t in tri.items()}
    dmask = {r: same_blk & t for r, t in tri.items()}
    k_refs = {False: kf_ref, True: kb_ref}
    st_refs = {False: sf_ref, True: sb_ref}
    sf_ref[...] = s0f_ref[...]
    sb_ref[...] = s0b_ref[...]

    def step(n, accumulate):
        units = []
        for u in range(group):
            for rev in (False, True):
                chunk = n * group + u
                if rev:
                    chunk = n_chunks - 1 - chunk
                units.append((rev, pl.ds(pl.multiple_of(chunk * c, c), c)))
        if scalar_decay:
            lfs = [jnp.broadcast_to(lg_ref[int(rev), 0:1, :], (c, dk)) for rev, _ in units]
        else:
            lfs = [(lfb_ref if rev else lff_ref)[rows, :] for rev, rows in units]
        css = [_chunk_running_sum(lf, tri3[rev]) for lf, (rev, _) in zip(lfs, units)]
        ops = [_chunk_operands(q_ref[rows, :], k_refs[rev][rows, :], cs, rev) for cs, (rev, rows) in zip(css, units)]
        vs = [v_ref[rows, :].astype(bf) for _, rows in units]
        diag = [lax.dot_general(p["qd"], p["kd"], _NT, preferred_element_type=jnp.float32) for p in ops]
        cross = [lax.dot_general(p["q_cat"], p["k_cat"], _NT, preferred_element_type=jnp.float32) for p in ops]
        kv = [lax.dot_general(v, p["k_state"], _TN, preferred_element_type=jnp.float32) for v, p in zip(vs, ops)]
        scores = [(jnp.where(dmask[rev], d, 0.0) + x).astype(bf) for d, x, (rev, _) in zip(diag, cross, units)]
        intra = [jnp.dot(s, v, preferred_element_type=jnp.float32) for s, v in zip(scores, vs)]
        for rev in (False, True):
            st = st_refs[rev][...]
            for i, (r, rows) in enumerate(units):
                if r != rev:
                    continue
                o = intra[i] + lax.dot_general(ops[i]["q_state"], st.astype(bf), _NT,
                                               preferred_element_type=jnp.float32)
                st = st * ops[i]["decay"] + kv[i]
                if accumulate:
                    o_ref[rows, :] += o
                else:
                    o_ref[rows, :] = o
            st_refs[rev][...] = st

    steps = n_chunks // group

    def first(n, carry):
        step(n, False)
        return carry

    def second(n, carry):
        step(n, True)
        return carry

    lax.fori_loop(0, steps // 2, first, 0)
    lax.fori_loop(steps // 2, steps, second, 0)

    tr = min(SCAN_EPILOGUE_ROWS, n_chunks * c)

    def epilogue(i, carry):
        rows = pl.ds(pl.multiple_of(i * tr, tr), tr)
        o = o_ref[rows, :]
        g = gate_ref[rows, :]
        y = o * lax.rsqrt(jnp.mean(o * o, axis=-1, keepdims=True) + RMS_EPS) * nw_ref[...]
        o_ref[rows, :] = y * (g * jax.nn.sigmoid(g))
        return carry

    lax.fori_loop(0, n_chunks * c // tr, epilogue, 0)


def _bidir_scan(q, k_f, k_b, v, lf_f, lf_b, gate, norm_w, s0_f, s0_b, n_heads, log_decay=None):
    b, l, hk = q.shape
    dk = hk // n_heads
    dv = v.shape[-1] // n_heads
    n_chunks = l // CHUNK
    group = min(SCAN_GROUP, n_chunks // 2)
    assert l % CHUNK == 0 and n_chunks % (2 * group) == 0
    scalar_decay = log_decay is not None
    seq_k = pl.BlockSpec((None, l, dk), lambda bi, hi: (bi, 0, hi))
    seq_v = pl.BlockSpec((None, l, dv), lambda bi, hi: (bi, 0, hi))
    state = pl.BlockSpec((None, None, dv, dk), lambda bi, hi: (bi, hi, 0, 0))
    if scalar_decay:
        lg = jnp.broadcast_to(log_decay.astype(jnp.float32)[:, :, None, None], (2, n_heads, 8, dk))
        decay_args = (lg,)
        decay_specs = [pl.BlockSpec((2, None, 8, dk), lambda bi, hi: (0, hi, 0, 0))]
    else:
        decay_args = (lf_f, lf_b)
        decay_specs = [seq_k, seq_k]
    return pl.pallas_call(
        functools.partial(_bidir_scan_body, n_chunks=n_chunks, group=group, scalar_decay=scalar_decay),
        grid=(b, n_heads),
        in_specs=[seq_k, seq_k, seq_k, seq_v] + decay_specs + [
            seq_v, pl.BlockSpec((1, dv), lambda bi, hi: (0, 0)), state, state],
        out_specs=[seq_v, state, state],
        out_shape=[jax.ShapeDtypeStruct((b, l, n_heads * dv), jnp.float32),
                   jax.ShapeDtypeStruct((b, n_heads, dv, dk), jnp.float32),
                   jax.ShapeDtypeStruct((b, n_heads, dv, dk), jnp.float32)],
        compiler_params=pltpu.CompilerParams(
            dimension_semantics=("parallel", "parallel"),
            vmem_limit_bytes=VMEM_LIMIT_BYTES),
        name="bidir_scan",
    )(q, k_f, k_b, v, *decay_args, gate, norm_w.astype(jnp.float32).reshape(1, dv), s0_f, s0_b)


def _hgrn2_forget(z, lb):
    log_f = jnp.logaddexp(z, jnp.log(jnp.maximum(lb, LB_FLOOR))) - jax.nn.softplus(z)
    key = (1.0 - lb) * jax.nn.sigmoid(-z)
    return log_f, key


def _even_mixer(h, w_in, w_out, lb, ha_norm_w, rb_log_gamma, rope, states):
    p = jnp.einsum('bld,de->ble', h, w_in)
    qa, fa_f, fa_b, ia, ga, qb, kb, vb, gb = _split(p, EVEN_SIZES)
    qa = qa * HA_DK ** -0.5
    lf_f, ka_f = _hgrn2_forget(fa_f, lb[0])
    lf_b, ka_b = _hgrn2_forget(fa_b, lb[1])
    ya, sa_f, sa_b = _bidir_scan(qa, ka_f, ka_b, ia, lf_f, lf_b, ga, ha_norm_w, states[0][0], states[0][1],
                                 HA_HEADS)
    qb = qb * RB_DK ** -0.5
    if rope is not None:
        qb = _apply_rope(qb, RB_HEADS, *rope)
        kb = _apply_rope(kb, RB_HEADS, *rope)
    yb, sb_f, sb_b = _bidir_scan(qb, kb, kb, vb, None, None, gb, jnp.ones((RB_DV,), jnp.float32),
                                 states[1][0], states[1][1], RB_HEADS, log_decay=rb_log_gamma)
    y = jnp.concatenate([ya, yb], axis=-1)
    return jnp.einsum('ble,ed->bld', y, w_out), ((sa_f, sa_b), (sb_f, sb_b))


def _odd_mixer(h, w_in, w_out, w2, b2, norm_w, states):
    p = jnp.einsum('bld,de->ble', h, w_in)
    qc, kc, vc, gc, lr = _split(p, ODD_SIZES)
    q = qc * GC_DK ** -0.5
    lr_f, lr_b = jnp.split(lr, 2, axis=-1)
    lf_f = jax.nn.log_sigmoid(lr_f @ w2[0] + b2[0]) / GC_TAU
    lf_b = jax.nn.log_sigmoid(lr_b @ w2[1] + b2[1]) / GC_TAU
    y, s_f, s_b = _bidir_scan(q, kc, kc, vc, lf_f, lf_b, gc, norm_w, states[0], states[1], GC_HEADS)
    return jnp.einsum('ble,ed->bld', y, w_out), (s_f, s_b)


def _expert_ffn_body(x_ref, gate_ref, wg_ref, wu_ref, wd_ref, out_ref, hid_ref):
    tb, cap, d = x_ref.shape
    ff = wg_ref.shape[-1]
    x = x_ref[...].reshape(tb * cap, d)
    for j in range(ff // FFN_FF_TILE):
        cols = slice(j * FFN_FF_TILE, (j + 1) * FFN_FF_TILE)
        g = jnp.dot(x, wg_ref[:, cols], preferred_element_type=jnp.float32)
        u = jnp.dot(x, wu_ref[:, cols], preferred_element_type=jnp.float32)
        hid_ref[:, cols] = (g * jax.nn.sigmoid(g) * u).astype(jnp.bfloat16)
    y = jnp.dot(hid_ref[...], wd_ref[...], preferred_element_type=jnp.float32)
    out_ref[...] = (y.reshape(tb, cap, d) * gate_ref[...]).astype(out_ref.dtype)


def _expert_ffn(xs, gate, w_gate, w_up, w_down):
    b, e, cap, d = xs.shape
    ff = w_gate.shape[-1]
    tb = max(1, min(b, FFN_ROWS_PER_STEP // cap))
    assert b % tb == 0 and ff % FFN_FF_TILE == 0
    return pl.pallas_call(
        _expert_ffn_body,
        grid=(e, b // tb),
        in_specs=[
            pl.BlockSpec((tb, None, cap, d), lambda ei, bi: (bi, ei, 0, 0)),
            pl.BlockSpec((tb, None, cap, 1), lambda ei, bi: (bi, ei, 0, 0)),
            pl.BlockSpec((None, d, ff), lambda ei, bi: (ei, 0, 0)),
            pl.BlockSpec((None, d, ff), lambda ei, bi: (ei, 0, 0)),
            pl.BlockSpec((None, ff, d), lambda ei, bi: (ei, 0, 0)),
        ],
        out_specs=pl.BlockSpec((tb, None, cap, d), lambda ei, bi: (bi, ei, 0, 0)),
        out_shape=jax.ShapeDtypeStruct((b, e, cap, d), jnp.bfloat16),
        scratch_shapes=[pltpu.VMEM((tb * cap, ff), jnp.bfloat16)],
        compiler_params=pltpu.CompilerParams(
            dimension_semantics=("parallel", "parallel"),
            vmem_limit_bytes=VMEM_LIMIT_BYTES),
        name="expert_ffn",
    )(xs, gate, w_gate, w_up, w_down)


def _combine_ln_body(idx_ref, ys_ref, x_ref, g_ref, w_ref, b_ref, out_ref):
    tr, d = x_ref.shape
    e, cap, _ = ys_ref.shape
    r0 = pl.program_id(1) * tr
    tokens = lax.broadcasted_iota(jnp.int32, (tr, e * cap), 0) + r0
    onehot = jnp.where(tokens == idx_ref[...], 1.0, 0.0).astype(jnp.bfloat16)
    ffn = jnp.dot(onehot, ys_ref[...].reshape(e * cap, d), preferred_element_type=jnp.float32)
    z = DN_ALPHA * x_ref[...] + g_ref[...] * ffn
    mu = jnp.mean(z, axis=-1, keepdims=True)
    zc = z - mu
    var = jnp.mean(zc * zc, axis=-1, keepdims=True)
    out_ref[...] = zc * lax.rsqrt(var + LN_EPS) * w_ref[...] + b_ref[...]


def _combine_ln(ys, idx, x, g, w, b):
    bsz, e, cap, d = ys.shape
    n = x.shape[1]
    tr = min(n, COMBINE_ROWS)
    assert n % tr == 0
    per_sample = g.shape[0] == bsz
    return pl.pallas_call(
        _combine_ln_body,
        grid=(bsz, n // tr),
        in_specs=[
            pl.BlockSpec((None, 1, e * cap), lambda bi, ri: (bi, 0, 0)),
            pl.BlockSpec((None, e, cap, d), lambda bi, ri: (bi, 0, 0, 0)),
            pl.BlockSpec((None, tr, d), lambda bi, ri: (bi, ri, 0)),
            pl.BlockSpec((None, 1, d), (lambda bi, ri: (bi, 0, 0)) if per_sample else (lambda bi, ri: (0, 0, 0))),
            pl.BlockSpec((1, d), lambda bi, ri: (0, 0)),
            pl.BlockSpec((1, d), lambda bi, ri: (0, 0)),
        ],
        out_specs=pl.BlockSpec((None, tr, d), lambda bi, ri: (bi, ri, 0)),
        out_shape=jax.ShapeDtypeStruct((bsz, n, d), jnp.float32),
        compiler_params=pltpu.CompilerParams(
            dimension_semantics=("parallel", "parallel"),
            vmem_limit_bytes=VMEM_LIMIT_BYTES),
        name="combine_ln",
    )(idx.reshape(bsz, 1, e * cap), ys, x, g, w.reshape(1, d), b.reshape(1, d))


def _ec_ffn_ln(x, sh, sc, g, w_router, w_gate, w_up, w_down, ln_w, ln_b):
    b, n, d = x.shape
    h = x * (1.0 + sc) + sh
    cap = EC_CAPACITY_FACTOR * n // N_EXPERTS
    aff = jax.nn.softmax(jnp.einsum('bnd,de->bne', h, w_router), axis=-1)
    gate, idx = lax.top_k(jnp.swapaxes(aff, 1, 2), cap)
    xs = jax.vmap(lambda hb, ib: hb[ib])(h.astype(jnp.bfloat16), idx)
    ys = _expert_ffn(xs, gate[..., None], w_gate, w_up, w_down)
    return _combine_ln(ys, idx, x, g, ln_w, ln_b)


def kernel(x, c, ctx, c_ctx, ada_w, ada_b, ln_w, ln_b, even_w_in, even_w_out, ha_lb, ha_norm,
           rb_decay, odd_w_in, odd_w_out, gc_w2, gc_b2, gc_norm, router_w, exp_w_gate, exp_w_up,
           exp_w_down):
    n_lat = x.shape[1]
    rows = n_lat // GRID_W
    rope = _axial_rope(rows)
    b_ctx = ctx.shape[0]
    lb_p = jax.nn.softmax(ha_lb, axis=0)
    lb_all = jnp.cumsum(lb_p, axis=0) - lb_p[0]
    cond_lat = jax.nn.silu(c)
    cond_ctx = jax.nn.silu(c_ctx)

    for l in range(DEPTH):
        last = l == DEPTH - 1
        mod_lat = (cond_lat @ ada_w[l] + ada_b[l])[:, None, :]
        mod_ctx = (cond_ctx @ ada_w[l] + ada_b[l])[None, None, :]
        sh1, sc1, g1, sh2, sc2, g2 = jnp.split(mod_lat, 6, axis=-1)
        csh1, csc1, cg1, csh2, csc2, cg2 = jnp.split(mod_ctx, 6, axis=-1)
        h_lat = x * (1.0 + sc1) + sh1
        h_ctx = ctx * (1.0 + csc1) + csh1

        j = l // 2
        if l % 2 == 0:
            z_a = jnp.zeros((b_ctx, HA_HEADS, HA_DV, HA_DK), jnp.float32)
            z_b = jnp.zeros((b_ctx, RB_HEADS, RB_DV, RB_DK), jnp.float32)
            log_gamma = jax.nn.log_sigmoid(rb_decay[j])
            y_ctx, st = _even_mixer(h_ctx, even_w_in[j], even_w_out[j], lb_all[j], ha_norm[j],
                                    log_gamma, None, ((z_a, z_a), (z_b, z_b)))
            y_lat, _ = _even_mixer(h_lat, even_w_in[j], even_w_out[j], lb_all[j], ha_norm[j],
                                   log_gamma, rope, st)
        else:
            z_c = jnp.zeros((b_ctx, GC_HEADS, GC_DV, GC_DK), jnp.float32)
            y_ctx, st = _odd_mixer(h_ctx, odd_w_in[j], odd_w_out[j], gc_w2[j], gc_b2[j], gc_norm[j],
                                   (z_c, z_c))
            y_lat, _ = _odd_mixer(h_lat, odd_w_in[j], odd_w_out[j], gc_w2[j], gc_b2[j], gc_norm[j], st)

        ffn = (router_w[l], exp_w_gate[l].astype(jnp.bfloat16), exp_w_up[l].astype(jnp.bfloat16),
               exp_w_down[l].astype(jnp.bfloat16))
        x = _layer_norm(DN_ALPHA * x + g1 * y_lat, ln_w[l, 0], ln_b[l, 0])
        x = _ec_ffn_ln(x, sh2, sc2, g2, *ffn, ln_w[l, 1], ln_b[l, 1])
        if not last:
            ctx = _layer_norm(DN_ALPHA * ctx + cg1 * y_ctx, ln_w[l, 0], ln_b[l, 0])
            ctx = _ec_ffn_ln(ctx, csh2, csc2, cg2, *ffn, ln_w[l, 1], ln_b[l, 1])
    return x
```

```python
import functools

import jax
import jax.numpy as jnp
import numpy as np
from jax import lax
from jax.experimental import pallas as pl
from jax.experimental.pallas import tpu as pltpu

D_MODEL = 1024
DEPTH = 4
GRID_W = 64

HA_HEADS = 4
HA_DK = 128
HA_DV = 128
RB_HEADS = 4
RB_DK = 128
RB_DV = 128
GC_HEADS = 4
GC_DK = 128
GC_DV = 256
GC_RANK = 16
GC_TAU = 16.0
N_EXPERTS = 16
EXPERT_FF = 2816
EC_CAPACITY_FACTOR = 2

CHUNK = 64
SUB = 16
N_SUB = CHUNK // SUB
ROPE_BASE = 10000.0
LN_EPS = 1e-5
RMS_EPS = 1e-6
LB_FLOOR = 1e-30
DN_ALPHA = (2 * DEPTH) ** 0.25

HA_KEY = HA_HEADS * HA_DK
HA_VAL = HA_HEADS * HA_DV
RB_KEY = RB_HEADS * RB_DK
RB_VAL = RB_HEADS * RB_DV
GC_KEY = GC_HEADS * GC_DK
GC_VAL = GC_HEADS * GC_DV
EVEN_SIZES = (HA_KEY, HA_KEY, HA_KEY, HA_VAL, HA_VAL, RB_KEY, RB_KEY, RB_VAL, RB_VAL)
ODD_SIZES = (GC_KEY, GC_KEY, GC_VAL, GC_VAL, 2 * GC_RANK)

VMEM_LIMIT_BYTES = 56 * 1024 * 1024
FFN_ROWS_PER_STEP = 1024
FFN_FF_TILE = 256
SCAN_EPILOGUE_ROWS = 256
COMBINE_ROWS = 256
PROJ_ROWS = 256
SCAN_GROUP = 8

_NT = (((1,), (1,)), ((), ()))
_TN = (((0,), (0,)), ((), ()))


def _axial_rope(rows):
    r_idx, c_idx = jnp.meshgrid(jnp.arange(rows), jnp.arange(GRID_W), indexing='ij')
    n_freq = RB_DK // 4
    freq = ROPE_BASE ** (-jnp.arange(n_freq, dtype=jnp.float32) / n_freq)
    ang = jnp.concatenate([r_idx.reshape(-1, 1).astype(jnp.float32) * freq,
                           c_idx.reshape(-1, 1).astype(jnp.float32) * freq], axis=-1)
    return jnp.cos(ang), jnp.sin(ang)


def _chunk_running_sum(lf, tri3):
    bf = jnp.bfloat16
    hi = lf.astype(bf)
    r1 = lf - hi.astype(jnp.float32)
    mid = r1.astype(bf)
    lo = (r1 - mid.astype(jnp.float32)).astype(bf)
    return jnp.dot(tri3, jnp.concatenate([hi, mid, lo], axis=0), preferred_element_type=jnp.float32)


def _chunk_operands(q, k, cs, rev):
    bf = jnp.bfloat16
    dk = q.shape[-1]
    zero_row = jnp.zeros((1, dk), jnp.float32)
    if not rev:
        ends = [cs[SUB * a + SUB - 1:SUB * a + SUB, :] for a in range(N_SUB)]
        starts = [zero_row] + ends[:-1]
        mids = [cs[SUB * a + SUB // 2 - 1:SUB * a + SUB // 2, :] for a in range(N_SUB)]
        order = list(range(N_SUB))
    else:
        ends = [cs[SUB * a:SUB * a + 1, :] for a in range(N_SUB)]
        starts = ends[1:] + [zero_row]
        mids = [cs[SUB * a + SUB // 2:SUB * a + SUB // 2 + 1, :] for a in range(N_SUB)]
        order = list(range(N_SUB - 1, -1, -1))
    last = ends[order[-1]]

    def per_block(rows):
        return jnp.concatenate([jnp.broadcast_to(r, (SUB, dk)) for r in rows], axis=0)

    mid_b, start_b, end_b = per_block(mids), per_block(starts), per_block(ends)
    qd = q * jnp.exp(cs - mid_b)
    kd = k * jnp.exp(mid_b - cs)
    q_in = q * jnp.exp(cs - start_b)
    k_out = k * jnp.exp(end_b - cs)
    q_state = q_in * jnp.exp(start_b)
    k_state = k_out * jnp.exp(last - end_b)

    def blk(t, a):
        return t[SUB * a:SUB * (a + 1), :]

    zeros_blk = jnp.zeros((SUB, dk), jnp.float32)
    q_cat, k_cat = [], []
    for s in range(N_SUB - 1):
        src = order[s]
        q_rows = [zeros_blk] * N_SUB
        k_rows = [zeros_blk] * N_SUB
        k_rows[src] = blk(k_out, src)
        for t in range(s + 1, N_SUB):
            tgt = order[t]
            piece = blk(q_in, tgt)
            if t > s + 1:
                piece = piece * jnp.exp(jnp.minimum(starts[tgt] - ends[src], 0.0))
            q_rows[tgt] = piece
        q_cat.append(jnp.concatenate(q_rows, axis=0))
        k_cat.append(jnp.concatenate(k_rows, axis=0))
    return dict(qd=qd.astype(bf), kd=kd.astype(bf),
                q_cat=jnp.concatenate(q_cat, axis=1).astype(bf), k_cat=jnp.concatenate(k_cat, axis=1).astype(bf),
                q_state=q_state.astype(bf), k_state=k_state.astype(bf), decay=jnp.exp(last))


def _bidir_scan_body(*refs, n_chunks, group, scalar_decay):
    if scalar_decay:
        q_ref, kf_ref, kb_ref, v_ref, lg_ref, gate_ref, nw_ref, s0f_ref, s0b_ref, o_ref, sf_ref, sb_ref = refs
    else:
        (q_ref, kf_ref, kb_ref, v_ref, lff_ref, lfb_ref, gate_ref, nw_ref, s0f_ref, s0b_ref,
         o_ref, sf_ref, sb_ref) = refs
    c = CHUNK
    bf = jnp.bfloat16
    dk = q_ref.shape[-1]
    row = lax.broadcasted_iota(jnp.int32, (c, c), 0)
    col = lax.broadcasted_iota(jnp.int32, (c, c), 1)
    same_blk = (row // SUB) == (col // SUB)
    tri = {False: (col <= row), True: (col >= row)}
    tri3 = {r: jnp.concatenate([t.astype(bf)] * 3, axis=1) for r, t in tri.items()}
    dmask = {r: same_blk & t for r, t in tri.items()}
    k_refs = {False: kf_ref, True: kb_ref}
    st_refs = {False: sf_ref, True: sb_ref}
    sf_ref[...] = s0f_ref[...]
    sb_ref[...] = s0b_ref[...]

    def step(n, accumulate):
        units = []
        for u in range(group):
            for rev in (False, True):
                chunk = n * group + u
                if rev:
                    chunk = n_chunks - 1 - chunk
                units.append((rev, pl.ds(pl.multiple_of(chunk * c, c), c)))
        if scalar_decay:
            lfs = [jnp.broadcast_to(lg_ref[int(rev), 0:1, :], (c, dk)) for rev, _ in units]
        else:
            lfs = [(lfb_ref if rev else lff_ref)[rows, :] for rev, rows in units]
        css = [_chunk_running_sum(lf, tri3[rev]) for lf, (rev, _) in zip(lfs, units)]
        ops = [_chunk_operands(q_ref[rows, :], k_refs[rev][rows, :], cs, rev) for cs, (rev, rows) in zip(css, units)]
        vs = [v_ref[rows, :].astype(bf) for _, rows in units]
        diag = [lax.dot_general(p["qd"], p["kd"], _NT, preferred_element_type=jnp.float32) for p in ops]
        cross = [lax.dot_general(p["q_cat"], p["k_cat"], _NT, preferred_element_type=jnp.float32) for p in ops]
        kv = [lax.dot_general(v, p["k_state"], _TN, preferred_element_type=jnp.float32) for v, p in zip(vs, ops)]
        scores = [(jnp.where(dmask[rev], d, 0.0) + x).astype(bf) for d, x, (rev, _) in zip(diag, cross, units)]
        intra = [jnp.dot(s, v, preferred_element_type=jnp.float32) for s, v in zip(scores, vs)]
        for rev in (False, True):
            st = st_refs[rev][...]
            for i, (r, rows) in enumerate(units):
                if r != rev:
                    continue
                o = intra[i] + lax.dot_general(ops[i]["q_state"], st.astype(bf), _NT,
                                               preferred_element_type=jnp.float32)
                st = st * ops[i]["decay"] + kv[i]
                if accumulate:
                    o_ref[rows, :] += o
                else:
                    o_ref[rows, :] = o
            st_refs[rev][...] = st

    steps = n_chunks // group

    def first(n, carry):
        step(n, False)
        return carry

    def second(n, carry):
        step(n, True)
        return carry

    lax.fori_loop(0, steps // 2, first, 0)
    lax.fori_loop(steps // 2, steps, second, 0)

    tr = min(SCAN_EPILOGUE_ROWS, n_chunks * c)

    def epilogue(i, carry):
        rows = pl.ds(pl.multiple_of(i * tr, tr), tr)
        o = o_ref[rows, :]
        g = gate_ref[rows, :]
        y = o * lax.rsqrt(jnp.mean(o * o, axis=-1, keepdims=True) + RMS_EPS) * nw_ref[...]
        o_ref[rows, :] = y * (g * jax.nn.sigmoid(g))
        return carry

    lax.fori_loop(0, n_chunks * c // tr, epilogue, 0)


def _bidir_scan(q, k_f, k_b, v, lf_f, lf_b, gate, norm_w, s0_f, s0_b, n_heads, log_decay=None):
    b, l, hk = q.shape
    dk = hk // n_heads
    dv = v.shape[-1] // n_heads
    n_chunks = l // CHUNK
    group = min(SCAN_GROUP, n_chunks // 2)
    assert l % CHUNK == 0 and n_chunks % (2 * group) == 0
    scalar_decay = log_decay is not None
    seq_k = pl.BlockSpec((None, l, dk), lambda bi, hi: (bi, 0, hi))
    seq_v = pl.BlockSpec((None, l, dv), lambda bi, hi: (bi, 0, hi))
    state = pl.BlockSpec((None, None, dv, dk), lambda bi, hi: (bi, hi, 0, 0))
    if scalar_decay:
        lg = jnp.broadcast_to(log_decay.astype(jnp.float32)[:, :, None, None], (2, n_heads, 8, dk))
        decay_args = (lg,)
        decay_specs = [pl.BlockSpec((2, None, 8, dk), lambda bi, hi: (0, hi, 0, 0))]
    else:
        decay_args = (lf_f, lf_b)
        decay_specs = [seq_k, seq_k]
    return pl.pallas_call(
        functools.partial(_bidir_scan_body, n_chunks=n_chunks, group=group, scalar_decay=scalar_decay),
        grid=(b, n_heads),
        in_specs=[seq_k, seq_k, seq_k, seq_v] + decay_specs + [
            seq_v, pl.BlockSpec((1, dv), lambda bi, hi: (0, 0)), state, state],
        out_specs=[seq_v, state, state],
        out_shape=[jax.ShapeDtypeStruct((b, l, n_heads * dv), jnp.float32),
                   jax.ShapeDtypeStruct((b, n_heads, dv, dk), jnp.float32),
                   jax.ShapeDtypeStruct((b, n_heads, dv, dk), jnp.float32)],
        compiler_params=pltpu.CompilerParams(
            dimension_semantics=("parallel", "parallel"),
            vmem_limit_bytes=VMEM_LIMIT_BYTES),
        name="bidir_scan",
    )(q, k_f, k_b, v, *decay_args, gate, norm_w.astype(jnp.float32).reshape(1, dv), s0_f, s0_b)


def _softplus(x):
    return jnp.maximum(x, 0.0) + jnp.log1p(jnp.exp(-jnp.abs(x)))


def _modulated_bf16(x_ref, sc_ref, sh_ref):
    return (x_ref[...] * (1.0 + sc_ref[...]) + sh_ref[...]).astype(jnp.bfloat16)


def _rope_heads(t, cos2, sin2, n_heads):
    dk = t.shape[-1] // n_heads
    out = []
    for hh in range(n_heads):
        th = t[:, hh * dk:(hh + 1) * dk]
        out.append(th * cos2 + pltpu.roll(th, dk // 2, axis=1) * sin2)
    return jnp.concatenate(out, axis=-1)


def _even_in_body(*refs, use_rope):
    if use_rope:
        x_ref, sc_ref, sh_ref, w_ref, loglb_ref, omlb_ref, cos_ref, sin_ref = refs[:8]
    else:
        x_ref, sc_ref, sh_ref, w_ref, loglb_ref, omlb_ref = refs[:6]
    (qa_ref, kaf_ref, kab_ref, lff_ref, lfb_ref, ia_ref, ga_ref, qb_ref, kb_ref, vb_ref, gb_ref) = refs[-11:]
    h = _modulated_bf16(x_ref, sc_ref, sh_ref)
    offs = np.cumsum((0,) + EVEN_SIZES).tolist()

    def seg(i):
        return jnp.dot(h, w_ref[:, offs[i]:offs[i + 1]], preferred_element_type=jnp.float32)

    qa_ref[...] = seg(0) * HA_DK ** -0.5
    for d, (lf_ref, key_ref) in enumerate(((lff_ref, kaf_ref), (lfb_ref, kab_ref))):
        z = seg(1 + d)
        log_lb = loglb_ref[d:d + 1, :]
        lse = jnp.maximum(z, log_lb) + jnp.log1p(jnp.exp(-jnp.abs(z - log_lb)))
        lf_ref[...] = lse - _softplus(z)
        key_ref[...] = omlb_ref[d:d + 1, :] * jax.nn.sigmoid(-z)
    ia_ref[...] = seg(3).astype(ia_ref.dtype)
    ga_ref[...] = seg(4)
    qb = seg(5) * RB_DK ** -0.5
    kb = seg(6)
    if use_rope:
        qb = _rope_heads(qb, cos_ref[...], sin_ref[...], RB_HEADS)
        kb = _rope_heads(kb, cos_ref[...], sin_ref[...], RB_HEADS)
    qb_ref[...] = qb
    kb_ref[...] = kb
    vb_ref[...] = seg(7).astype(vb_ref.dtype)
    gb_ref[...] = seg(8)


def _row_tiled_call(body, name, x, sc, sh, consts, seq_inputs, out_widths, out_dtypes):
    b, l, d = x.shape
    tm = min(l, PROJ_ROWS)
    assert l % tm == 0

    def mod_spec(m):
        if m.shape[0] == b:
            return pl.BlockSpec((None, 1, d), lambda bi, ri: (bi, 0, 0))
        return pl.BlockSpec((None, 1, d), lambda bi, ri: (0, 0, 0))

    def const_spec(a):
        return pl.BlockSpec(a.shape, lambda bi, ri, nd=a.ndim: (0,) * nd)

    return pl.pallas_call(
        body,
        grid=(b, l // tm),
        in_specs=[pl.BlockSpec((None, tm, d), lambda bi, ri: (bi, ri, 0)), mod_spec(sc), mod_spec(sh)]
        + [const_spec(a) for a in consts]
        + [pl.BlockSpec((tm, a.shape[-1]), lambda bi, ri: (ri, 0)) for a in seq_inputs],
        out_specs=[pl.BlockSpec((None, tm, w), lambda bi, ri: (bi, ri, 0)) for w in out_widths],
        out_shape=[jax.ShapeDtypeStruct((b, l, w), dt) for w, dt in zip(out_widths, out_dtypes)],
        compiler_params=pltpu.CompilerParams(
            dimension_semantics=("parallel", "parallel"),
            vmem_limit_bytes=VMEM_LIMIT_BYTES),
        name=name,
    )(x, sc, sh, *consts, *seq_inputs)


def _even_mixer(x, sc, sh, w_in, lb, ha_norm_w, rb_log_gamma, rope, states):
    f32, bf = jnp.float32, jnp.bfloat16
    consts = [w_in.astype(bf), jnp.log(jnp.maximum(lb, LB_FLOOR)), 1.0 - lb]
    seq = []
    if rope is not None:
        cos, sin = rope
        seq = [jnp.concatenate([cos, cos], axis=-1), jnp.concatenate([-sin, sin], axis=-1)]
    qa, ka_f, ka_b, lf_f, lf_b, ia, ga, qb, kb, vb, gb = _row_tiled_call(
        functools.partial(_even_in_body, use_rope=rope is not None), "even_in_proj", x, sc, sh, consts, seq,
        (HA_KEY,) * 5 + (HA_VAL, HA_VAL, RB_KEY, RB_KEY, RB_VAL, RB_VAL),
        (f32,) * 5 + (bf, f32, f32, f32, bf, f32))
    ya, sa_f, sa_b = _bidir_scan(qa, ka_f, ka_b, ia, lf_f, lf_b, ga, ha_norm_w, states[0][0], states[0][1],
                                 HA_HEADS)
    yb, sb_f, sb_b = _bidir_scan(qb, kb, kb, vb, None, None, gb, jnp.ones((RB_DV,), f32),
                                 states[1][0], states[1][1], RB_HEADS, log_decay=rb_log_gamma)
    return (ya, yb), ((sa_f, sa_b), (sb_f, sb_b))


def _odd_in_body(x_ref, sc_ref, sh_ref, w_ref, w2_ref, b2_ref, q_ref, k_ref, v_ref, g_ref, lff_ref, lfb_ref):
    h = _modulated_bf16(x_ref, sc_ref, sh_ref)
    offs = np.cumsum((0,) + ODD_SIZES[:4]).tolist()

    def seg(lo, hi):
        return jnp.dot(h, w_ref[:, lo:hi], preferred_element_type=jnp.float32)

    q_ref[...] = seg(offs[0], offs[1]) * GC_DK ** -0.5
    k_ref[...] = seg(offs[1], offs[2])
    v_ref[...] = seg(offs[2], offs[3]).astype(v_ref.dtype)
    g_ref[...] = seg(offs[3], offs[4])
    lr = seg(offs[4], w_ref.shape[-1]).astype(jnp.bfloat16)
    for d, lf_ref in enumerate((lff_ref, lfb_ref)):
        zz = jnp.dot(lr, w2_ref[d], preferred_element_type=jnp.float32) + b2_ref[d:d + 1, :]
        lf_ref[...] = -_softplus(-zz) / GC_TAU


def _odd_mixer(x, sc, sh, w_in, w2, b2, norm_w, states):
    f32, bf = jnp.float32, jnp.bfloat16
    lane = 128
    pad = lane - 2 * GC_RANK
    w_pad = jnp.pad(w_in, ((0, 0), (0, pad))).astype(bf)
    w2_pad = jnp.stack([jnp.pad(w2[0], ((0, lane - GC_RANK), (0, 0))),
                        jnp.pad(w2[1], ((GC_RANK, lane - 2 * GC_RANK), (0, 0)))]).astype(bf)
    q, k, v, g, lf_f, lf_b = _row_tiled_call(
        _odd_in_body, "odd_in_proj", x, sc, sh, [w_pad, w2_pad, b2], [],
        (GC_KEY, GC_KEY, GC_VAL, GC_VAL, GC_KEY, GC_KEY), (f32, f32, bf, f32, f32, f32))
    y, s_f, s_b = _bidir_scan(q, k, k, v, lf_f, lf_b, g, norm_w, states[0], states[1], GC_HEADS)
    return (y,), (s_f, s_b)


def _mixer_out_body(*refs, n_y):
    y_refs = refs[:n_y]
    (wo_ref, x_ref, g1_ref, lnw_ref, lnb_ref, sc2_ref, sh2_ref, wr_ref, x1_ref, h2_ref, aff_ref) = refs[n_y:]
    mix = None
    lo = 0
    for y_ref in y_refs:
        hi = lo + y_ref.shape[-1]
        part = jnp.dot(y_ref[...].astype(jnp.bfloat16), wo_ref[lo:hi, :], preferred_element_type=jnp.float32)
        mix = part if mix is None else mix + part
        lo = hi
    z = DN_ALPHA * x_ref[...] + g1_ref[...] * mix
    mu = jnp.mean(z, axis=-1, keepdims=True)
    zc = z - mu
    var = jnp.mean(zc * zc, axis=-1, keepdims=True)
    x1 = zc * lax.rsqrt(var + LN_EPS) * lnw_ref[...] + lnb_ref[...]
    x1_ref[...] = x1
    h2 = (x1 * (1.0 + sc2_ref[...]) + sh2_ref[...]).astype(jnp.bfloat16)
    h2_ref[...] = h2
    logits = lax.dot_general(wr_ref[...], h2, _NT, preferred_element_type=jnp.float32)
    ex = jnp.exp(logits - jnp.max(logits, axis=0, keepdims=True))
    aff_ref[...] = ex / jnp.sum(ex, axis=0, keepdims=True)


def _mixer_out(ys, w_out, x, g1, ln_w, ln_b, sc2, sh2, w_router):
    b, l, d = x.shape
    e = w_router.shape[-1]
    tm = min(l, PROJ_ROWS)
    assert l % tm == 0

    def mod_spec(m):
        if m.shape[0] == b:
            return pl.BlockSpec((None, 1, d), lambda bi, ri: (bi, 0, 0))
        return pl.BlockSpec((None, 1, d), lambda bi, ri: (0, 0, 0))

    def const_spec(a):
        return pl.BlockSpec(a.shape, lambda bi, ri, nd=a.ndim: (0,) * nd)

    def rows(w):
        return pl.BlockSpec((None, tm, w), lambda bi, ri: (bi, ri, 0))

    consts_a = [w_out.astype(jnp.bfloat16)]
    consts_b = [ln_w.reshape(1, d), ln_b.reshape(1, d)]
    wr_t = w_router.T.astype(jnp.bfloat16)
    return pl.pallas_call(
        functools.partial(_mixer_out_body, n_y=len(ys)),
        grid=(b, l // tm),
        in_specs=[rows(y.shape[-1]) for y in ys] + [const_spec(consts_a[0]), rows(d), mod_spec(g1)]
        + [const_spec(a) for a in consts_b] + [mod_spec(sc2), mod_spec(sh2), const_spec(wr_t)],
        out_specs=[rows(d), rows(d), pl.BlockSpec((None, e, tm), lambda bi, ri: (bi, 0, ri))],
        out_shape=[jax.ShapeDtypeStruct((b, l, d), jnp.float32), jax.ShapeDtypeStruct((b, l, d), jnp.bfloat16),
                   jax.ShapeDtypeStruct((b, e, l), jnp.float32)],
        compiler_params=pltpu.CompilerParams(
            dimension_semantics=("parallel", "parallel"),
            vmem_limit_bytes=VMEM_LIMIT_BYTES),
        name="mixer_out",
    )(*ys, *consts_a, x, g1, *consts_b, sc2, sh2, wr_t)


def _expert_ffn_body(x_ref, gate_ref, wg_ref, wu_ref, wd_ref, out_ref, hid_ref):
    tb, cap, d = x_ref.shape
    ff = wg_ref.shape[-1]
    x = x_ref[...].reshape(tb * cap, d)
    for j in range(ff // FFN_FF_TILE):
        cols = slice(j * FFN_FF_TILE, (j + 1) * FFN_FF_TILE)
        g = jnp.dot(x, wg_ref[:, cols], preferred_element_type=jnp.float32)
        u = jnp.dot(x, wu_ref[:, cols], preferred_element_type=jnp.float32)
        hid_ref[:, cols] = (g * jax.nn.sigmoid(g) * u).astype(jnp.bfloat16)
    y = jnp.dot(hid_ref[...], wd_ref[...], preferred_element_type=jnp.float32)
    out_ref[...] = (y.reshape(tb, cap, d) * gate_ref[...]).astype(out_ref.dtype)


def _expert_ffn(xs, gate, w_gate, w_up, w_down):
    b, e, cap, d = xs.shape
    ff = w_gate.shape[-1]
    tb = max(1, min(b, FFN_ROWS_PER_STEP // cap))
    assert b % tb == 0 and ff % FFN_FF_TILE == 0
    return pl.pallas_call(
        _expert_ffn_body,
        grid=(e, b // tb),
        in_specs=[
            pl.BlockSpec((tb, None, cap, d), lambda ei, bi: (bi, ei, 0, 0)),
            pl.BlockSpec((tb, None, cap, 1), lambda ei, bi: (bi, ei, 0, 0)),
            pl.BlockSpec((None, d, ff), lambda ei, bi: (ei, 0, 0)),
            pl.BlockSpec((None, d, ff), lambda ei, bi: (ei, 0, 0)),
            pl.BlockSpec((None, ff, d), lambda ei, bi: (ei, 0, 0)),
        ],
        out_specs=pl.BlockSpec((tb, None, cap, d), lambda ei, bi: (bi, ei, 0, 0)),
        out_shape=jax.ShapeDtypeStruct((b, e, cap, d), jnp.bfloat16),
        scratch_shapes=[pltpu.VMEM((tb * cap, ff), jnp.bfloat16)],
        compiler_params=pltpu.CompilerParams(
            dimension_semantics=("parallel", "parallel"),
            vmem_limit_bytes=VMEM_LIMIT_BYTES),
        name="expert_ffn",
    )(xs, gate, w_gate, w_up, w_down)


def _combine_ln_body(idx_ref, ys_ref, x_ref, g_ref, w_ref, b_ref, out_ref):
    tr, d = x_ref.shape
    e, cap, _ = ys_ref.shape
    r0 = pl.program_id(1) * tr
    tokens = lax.broadcasted_iota(jnp.int32, (tr, e * cap), 0) + r0
    onehot = jnp.where(tokens == idx_ref[...], 1.0, 0.0).astype(jnp.bfloat16)
    ffn = jnp.dot(onehot, ys_ref[...].reshape(e * cap, d), preferred_element_type=jnp.float32)
    z = DN_ALPHA * x_ref[...] + g_ref[...] * ffn
    mu = jnp.mean(z, axis=-1, keepdims=True)
    zc = z - mu
    var = jnp.mean(zc * zc, axis=-1, keepdims=True)
    out_ref[...] = zc * lax.rsqrt(var + LN_EPS) * w_ref[...] + b_ref[...]


def _combine_ln(ys, idx, x, g, w, b):
    bsz, e, cap, d = ys.shape
    n = x.shape[1]
    tr = min(n, COMBINE_ROWS)
    assert n % tr == 0
    per_sample = g.shape[0] == bsz
    return pl.pallas_call(
        _combine_ln_body,
        grid=(bsz, n // tr),
        in_specs=[
            pl.BlockSpec((None, 1, e * cap), lambda bi, ri: (bi, 0, 0)),
            pl.BlockSpec((None, e, cap, d), lambda bi, ri: (bi, 0, 0, 0)),
            pl.BlockSpec((None, tr, d), lambda bi, ri: (bi, ri, 0)),
            pl.BlockSpec((None, 1, d), (lambda bi, ri: (bi, 0, 0)) if per_sample else (lambda bi, ri: (0, 0, 0))),
            pl.BlockSpec((1, d), lambda bi, ri: (0, 0)),
            pl.BlockSpec((1, d), lambda bi, ri: (0, 0)),
        ],
        out_specs=pl.BlockSpec((None, tr, d), lambda bi, ri: (bi, ri, 0)),
        out_shape=jax.ShapeDtypeStruct((bsz, n, d), jnp.float32),
        compiler_params=pltpu.CompilerParams(
            dimension_semantics=("parallel", "parallel"),
            vmem_limit_bytes=VMEM_LIMIT_BYTES),
        name="combine_ln",
    )(idx.reshape(bsz, 1, e * cap), ys, x, g, w.reshape(1, d), b.reshape(1, d))


def _ec_ffn_ln(x, h, aff, g, w_gate, w_up, w_down, ln_w, ln_b):
    n = x.shape[1]
    cap = EC_CAPACITY_FACTOR * n // N_EXPERTS
    gate, idx = lax.top_k(aff, cap)
    xs = jax.vmap(lambda hb, ib: hb[ib])(h, idx)
    ys = _expert_ffn(xs, gate[..., None], w_gate, w_up, w_down)
    return _combine_ln(ys, idx, x, g, ln_w, ln_b)


def kernel(x, c, ctx, c_ctx, ada_w, ada_b, ln_w, ln_b, even_w_in, even_w_out, ha_lb, ha_norm,
           rb_decay, odd_w_in, odd_w_out, gc_w2, gc_b2, gc_norm, router_w, exp_w_gate, exp_w_up,
           exp_w_down):
    n_lat = x.shape[1]
    rows = n_lat // GRID_W
    rope = _axial_rope(rows)
    b_ctx = ctx.shape[0]
    lb_p = jax.nn.softmax(ha_lb, axis=0)
    lb_all = jnp.cumsum(lb_p, axis=0) - lb_p[0]
    cond_lat = jax.nn.silu(c)
    cond_ctx = jax.nn.silu(c_ctx)

    for l in range(DEPTH):
        last = l == DEPTH - 1
        mod_lat = (cond_lat @ ada_w[l] + ada_b[l])[:, None, :]
        mod_ctx = (cond_ctx @ ada_w[l] + ada_b[l])[None, None, :]
        sh1, sc1, g1, sh2, sc2, g2 = jnp.split(mod_lat, 6, axis=-1)
        csh1, csc1, cg1, csh2, csc2, cg2 = jnp.split(mod_ctx, 6, axis=-1)

        j = l // 2
        if l % 2 == 0:
            z_a = jnp.zeros((b_ctx, HA_HEADS, HA_DV, HA_DK), jnp.float32)
            z_b = jnp.zeros((b_ctx, RB_HEADS, RB_DV, RB_DK), jnp.float32)
            log_gamma = jax.nn.log_sigmoid(rb_decay[j])
            w_out = even_w_out[j]
            y_ctx, st = _even_mixer(ctx, csc1, csh1, even_w_in[j], lb_all[j], ha_norm[j], log_gamma, None,
                                    ((z_a, z_a), (z_b, z_b)))
            y_lat, _ = _even_mixer(x, sc1, sh1, even_w_in[j], lb_all[j], ha_norm[j], log_gamma, rope, st)
        else:
            z_c = jnp.zeros((b_ctx, GC_HEADS, GC_DV, GC_DK), jnp.float32)
            w_out = odd_w_out[j]
            y_ctx, st = _odd_mixer(ctx, csc1, csh1, odd_w_in[j], gc_w2[j], gc_b2[j], gc_norm[j], (z_c, z_c))
            y_lat, _ = _odd_mixer(x, sc1, sh1, odd_w_in[j], gc_w2[j], gc_b2[j], gc_norm[j], st)

        experts = (exp_w_gate[l].astype(jnp.bfloat16), exp_w_up[l].astype(jnp.bfloat16),
                   exp_w_down[l].astype(jnp.bfloat16))
        x, h2, aff = _mixer_out(y_lat, w_out, x, g1, ln_w[l, 0], ln_b[l, 0], sc2, sh2, router_w[l])
        x = _ec_ffn_ln(x, h2, aff, g2, *experts, ln_w[l, 1], ln_b[l, 1])
        if not last:
            ctx, h2, aff = _mixer_out(y_ctx, w_out, ctx, cg1, ln_w[l, 0], ln_b[l, 0], csc2, csh2, router_w[l])
            ctx = _ec_ffn_ln(ctx, h2, aff, cg2, *experts, ln_w[l, 1], ln_b[l, 1])
    return x
```

```python
import functools

import jax
import jax.numpy as jnp
import numpy as np
from jax import lax
from jax.experimental import pallas as pl
from jax.experimental.pallas import tpu as pltpu

D_MODEL = 1024
DEPTH = 4
GRID_W = 64

HA_HEADS = 4
HA_DK = 128
HA_DV = 128
RB_HEADS = 4
RB_DK = 128
RB_DV = 128
GC_HEADS = 4
GC_DK = 128
GC_DV = 256
GC_RANK = 16
GC_TAU = 16.0
N_EXPERTS = 16
EXPERT_FF = 2816
EC_CAPACITY_FACTOR = 2

CHUNK = 64
SUB = 16
N_SUB = CHUNK // SUB
ROPE_BASE = 10000.0
LN_EPS = 1e-5
RMS_EPS = 1e-6
LB_FLOOR = 1e-30
DN_ALPHA = (2 * DEPTH) ** 0.25

HA_KEY = HA_HEADS * HA_DK
HA_VAL = HA_HEADS * HA_DV
RB_KEY = RB_HEADS * RB_DK
RB_VAL = RB_HEADS * RB_DV
GC_KEY = GC_HEADS * GC_DK
GC_VAL = GC_HEADS * GC_DV
EVEN_SIZES = (HA_KEY, HA_KEY, HA_KEY, HA_VAL, HA_VAL, RB_KEY, RB_KEY, RB_VAL, RB_VAL)
ODD_SIZES = (GC_KEY, GC_KEY, GC_VAL, GC_VAL, 2 * GC_RANK)

VMEM_LIMIT_BYTES = 56 * 1024 * 1024
FFN_ROWS_PER_STEP = 1024
FFN_FF_TILE = 256
SCAN_EPILOGUE_ROWS = 256
COMBINE_ROWS = 256
PROJ_ROWS = 512
SCAN_GROUP = 8

_NT = (((1,), (1,)), ((), ()))
_TN = (((0,), (0,)), ((), ()))


def _axial_rope(rows):
    r_idx, c_idx = jnp.meshgrid(jnp.arange(rows), jnp.arange(GRID_W), indexing='ij')
    n_freq = RB_DK // 4
    freq = ROPE_BASE ** (-jnp.arange(n_freq, dtype=jnp.float32) / n_freq)
    ang = jnp.concatenate([r_idx.reshape(-1, 1).astype(jnp.float32) * freq,
                           c_idx.reshape(-1, 1).astype(jnp.float32) * freq], axis=-1)
    return jnp.cos(ang), jnp.sin(ang)


def _chunk_running_sum(lf, tri3):
    bf = jnp.bfloat16
    hi = lf.astype(bf)
    r1 = lf - hi.astype(jnp.float32)
    mid = r1.astype(bf)
    lo = (r1 - mid.astype(jnp.float32)).astype(bf)
    return jnp.dot(tri3, jnp.concatenate([hi, mid, lo], axis=0), preferred_element_type=jnp.float32)


def _chunk_operands(q, k, cs, rev):
    bf = jnp.bfloat16
    dk = q.shape[-1]
    zero_row = jnp.zeros((1, dk), jnp.float32)
    if not rev:
        ends = [cs[SUB * a + SUB - 1:SUB * a + SUB, :] for a in range(N_SUB)]
        starts = [zero_row] + ends[:-1]
        mids = [cs[SUB * a + SUB // 2 - 1:SUB * a + SUB // 2, :] for a in range(N_SUB)]
        order = list(range(N_SUB))
    else:
        ends = [cs[SUB * a:SUB * a + 1, :] for a in range(N_SUB)]
        starts = ends[1:] + [zero_row]
        mids = [cs[SUB * a + SUB // 2:SUB * a + SUB // 2 + 1, :] for a in range(N_SUB)]
        order = list(range(N_SUB - 1, -1, -1))
    last = ends[order[-1]]

    def per_block(rows):
        return jnp.concatenate([jnp.broadcast_to(r, (SUB, dk)) for r in rows], axis=0)

    mid_b, start_b, end_b = per_block(mids), per_block(starts), per_block(ends)
    qd = q * jnp.exp(cs - mid_b)
    kd = k * jnp.exp(mid_b - cs)
    q_in = q * jnp.exp(cs - start_b)
    k_out = k * jnp.exp(end_b - cs)
    q_state = q_in * jnp.exp(start_b)
    k_state = k_out * jnp.exp(last - end_b)

    def blk(t, a):
        return t[SUB * a:SUB * (a + 1), :]

    zeros_blk = jnp.zeros((SUB, dk), jnp.float32)
    q_cat, k_cat = [], []
    for s in range(N_SUB - 1):
        src = order[s]
        q_rows = [zeros_blk] * N_SUB
        k_rows = [zeros_blk] * N_SUB
        k_rows[src] = blk(k_out, src)
        for t in range(s + 1, N_SUB):
            tgt = order[t]
            piece = blk(q_in, tgt)
            if t > s + 1:
                piece = piece * jnp.exp(jnp.minimum(starts[tgt] - ends[src], 0.0))
            q_rows[tgt] = piece
        q_cat.append(jnp.concatenate(q_rows, axis=0))
        k_cat.append(jnp.concatenate(k_rows, axis=0))
    return dict(qd=qd.astype(bf), kd=kd.astype(bf),
                q_cat=jnp.concatenate(q_cat, axis=1).astype(bf), k_cat=jnp.concatenate(k_cat, axis=1).astype(bf),
                q_state=q_state.astype(bf), k_state=k_state.astype(bf), decay=jnp.exp(last))


def _bidir_scan_body(*refs, n_chunks, group, scalar_decay):
    if scalar_decay:
        q_ref, kf_ref, kb_ref, v_ref, lg_ref, gate_ref, nw_ref, s0f_ref, s0b_ref, o_ref, sf_ref, sb_ref = refs
    else:
        (q_ref, kf_ref, kb_ref, v_ref, lff_ref, lfb_ref, gate_ref, nw_ref, s0f_ref, s0b_ref,
         o_ref, sf_ref, sb_ref) = refs
    c = CHUNK
    bf = jnp.bfloat16
    dk = q_ref.shape[-1]
    row = lax.broadcasted_iota(jnp.int32, (c, c), 0)
    col = lax.broadcasted_iota(jnp.int32, (c, c), 1)
    same_blk = (row // SUB) == (col // SUB)
    tri = {False: (col <= row), True: (col >= row)}
    tri3 = {r: jnp.concatenate([t.astype(bf)] * 3, axis=1) for r, t in tri.items()}
    dmask = {r: same_blk & t for r, t in tri.items()}
    k_refs = {False: kf_ref, True: kb_ref}
    st_refs = {False: sf_ref, True: sb_ref}
    sf_ref[...] = s0f_ref[...]
    sb_ref[...] = s0b_ref[...]
    if scalar_decay:
        pos = lax.broadcasted_iota(jnp.int32, (c, dk), 0).astype(jnp.float32)
        dist = (row - col).astype(jnp.float32)
        pair_decay, q_edge, k_edge, chunk_decay = {}, {}, {}, {}
        for rev in (False, True):
            lg = lg_ref[int(rev), 0:1, :]
            to_entry = (c - pos) if rev else (pos + 1.0)
            to_exit = pos if rev else (c - 1.0 - pos)
            signed = -dist if rev else dist
            pair_decay[rev] = jnp.where(tri[rev], jnp.exp(lg[:, :c] * jnp.maximum(signed, 0.0)), 0.0)
            q_edge[rev] = jnp.exp(lg * to_entry)
            k_edge[rev] = jnp.exp(lg * to_exit)
            chunk_decay[rev] = jnp.exp(lg * float(c))

    def step(n, accumulate):
        units = []
        for u in range(group):
            for rev in (False, True):
                chunk = n * group + u
                if rev:
                    chunk = n_chunks - 1 - chunk
                units.append((rev, pl.ds(pl.multiple_of(chunk * c, c), c)))
        vs = [v_ref[rows, :].astype(bf) for _, rows in units]
        if scalar_decay:
            qs = [q_ref[rows, :] for _, rows in units]
            ks = [k_refs[rev][rows, :] for rev, rows in units]
            ops = [dict(q_state=(q * q_edge[rev]).astype(bf), k_state=(k * k_edge[rev]).astype(bf),
                        decay=chunk_decay[rev]) for q, k, (rev, _) in zip(qs, ks, units)]
            qk = [lax.dot_general(q.astype(bf), k.astype(bf), _NT, preferred_element_type=jnp.float32)
                  for q, k in zip(qs, ks)]
            scores = [(s * pair_decay[rev]).astype(bf) for s, (rev, _) in zip(qk, units)]
        else:
            lfs = [(lfb_ref if rev else lff_ref)[rows, :] for rev, rows in units]
            css = [_chunk_running_sum(lf, tri3[rev]) for lf, (rev, _) in zip(lfs, units)]
            ops = [_chunk_operands(q_ref[rows, :], k_refs[rev][rows, :], cs, rev)
                   for cs, (rev, rows) in zip(css, units)]
            diag = [lax.dot_general(p["qd"], p["kd"], _NT, preferred_element_type=jnp.float32) for p in ops]
            cross = [lax.dot_general(p["q_cat"], p["k_cat"], _NT, preferred_element_type=jnp.float32) for p in ops]
            scores = [(jnp.where(dmask[rev], d, 0.0) + x).astype(bf) for d, x, (rev, _) in zip(diag, cross, units)]
        kv = [lax.dot_general(v, p["k_state"], _TN, preferred_element_type=jnp.float32) for v, p in zip(vs, ops)]
        intra = [jnp.dot(s, v, preferred_element_type=jnp.float32) for s, v in zip(scores, vs)]
        for rev in (False, True):
            st = st_refs[rev][...]
            for i, (r, rows) in enumerate(units):
                if r != rev:
                    continue
                o = intra[i] + lax.dot_general(ops[i]["q_state"], st.astype(bf), _NT,
                                               preferred_element_type=jnp.float32)
                st = st * ops[i]["decay"] + kv[i]
                if accumulate:
                    o_ref[rows, :] += o
                else:
                    o_ref[rows, :] = o
            st_refs[rev][...] = st

    steps = n_chunks // group

    def first(n, carry):
        step(n, False)
        return carry

    def second(n, carry):
        step(n, True)
        return carry

    lax.fori_loop(0, steps // 2, first, 0)
    lax.fori_loop(steps // 2, steps, second, 0)

    tr = min(SCAN_EPILOGUE_ROWS, n_chunks * c)

    def epilogue(i, carry):
        rows = pl.ds(pl.multiple_of(i * tr, tr), tr)
        o = o_ref[rows, :]
        g = gate_ref[rows, :]
        y = o * lax.rsqrt(jnp.mean(o * o, axis=-1, keepdims=True) + RMS_EPS) * nw_ref[...]
        o_ref[rows, :] = y * (g * jax.nn.sigmoid(g))
        return carry

    lax.fori_loop(0, n_chunks * c // tr, epilogue, 0)


def _bidir_scan(q, k_f, k_b, v, lf_f, lf_b, gate, norm_w, s0_f, s0_b, n_heads, log_decay=None):
    b, l, hk = q.shape
    dk = hk // n_heads
    dv = v.shape[-1] // n_heads
    n_chunks = l // CHUNK
    group = min(SCAN_GROUP, n_chunks // 2)
    assert l % CHUNK == 0 and n_chunks % (2 * group) == 0
    scalar_decay = log_decay is not None
    seq_k = pl.BlockSpec((None, l, dk), lambda bi, hi: (bi, 0, hi))
    seq_v = pl.BlockSpec((None, l, dv), lambda bi, hi: (bi, 0, hi))
    state = pl.BlockSpec((None, None, dv, dk), lambda bi, hi: (bi, hi, 0, 0))
    if scalar_decay:
        lg = jnp.broadcast_to(log_decay.astype(jnp.float32)[:, :, None, None], (2, n_heads, 8, dk))
        decay_args = (lg,)
        decay_specs = [pl.BlockSpec((2, None, 8, dk), lambda bi, hi: (0, hi, 0, 0))]
    else:
        decay_args = (lf_f, lf_b)
        decay_specs = [seq_k, seq_k]
    return pl.pallas_call(
        functools.partial(_bidir_scan_body, n_chunks=n_chunks, group=group, scalar_decay=scalar_decay),
        grid=(b, n_heads),
        in_specs=[seq_k, seq_k, seq_k, seq_v] + decay_specs + [
            seq_v, pl.BlockSpec((1, dv), lambda bi, hi: (0, 0)), state, state],
        out_specs=[seq_v, state, state],
        out_shape=[jax.ShapeDtypeStruct((b, l, n_heads * dv), jnp.float32),
                   jax.ShapeDtypeStruct((b, n_heads, dv, dk), jnp.float32),
                   jax.ShapeDtypeStruct((b, n_heads, dv, dk), jnp.float32)],
        compiler_params=pltpu.CompilerParams(
            dimension_semantics=("parallel", "parallel"),
            vmem_limit_bytes=VMEM_LIMIT_BYTES),
        name="bidir_scan",
    )(q, k_f, k_b, v, *decay_args, gate, norm_w.astype(jnp.float32).reshape(1, dv), s0_f, s0_b)


def _log_sigmoid(x):
    return jnp.minimum(x, 0.0) - jnp.log(1.0 + jnp.exp(-jnp.abs(x)))


def _modulated_bf16(x_ref, sc_ref, sh_ref):
    return (x_ref[...] * (1.0 + sc_ref[...]) + sh_ref[...]).astype(jnp.bfloat16)


def _rope_heads(t, cos2, sin2, n_heads):
    dk = t.shape[-1] // n_heads
    out = []
    for hh in range(n_heads):
        th = t[:, hh * dk:(hh + 1) * dk]
        out.append(th * cos2 + pltpu.roll(th, dk // 2, axis=1) * sin2)
    return jnp.concatenate(out, axis=-1)


def _even_in_body(*refs, use_rope):
    if use_rope:
        x_ref, sc_ref, sh_ref, w_ref, lbf_ref, omlb_ref, cos_ref, sin_ref = refs[:8]
    else:
        x_ref, sc_ref, sh_ref, w_ref, lbf_ref, omlb_ref = refs[:6]
    (qa_ref, kaf_ref, kab_ref, lff_ref, lfb_ref, ia_ref, ga_ref, qb_ref, kb_ref, vb_ref, gb_ref) = refs[-11:]
    h = _modulated_bf16(x_ref, sc_ref, sh_ref)
    offs = np.cumsum((0,) + EVEN_SIZES).tolist()

    def seg(i):
        return jnp.dot(h, w_ref[:, offs[i]:offs[i + 1]], preferred_element_type=jnp.float32)

    qa_ref[...] = seg(0) * HA_DK ** -0.5
    for d, (lf_ref, key_ref) in enumerate(((lff_ref, kaf_ref), (lfb_ref, kab_ref))):
        z = seg(1 + d)
        e = jnp.exp(-jnp.abs(z))
        big = 1.0 / (1.0 + e)
        small = e * big
        pos = z >= 0.0
        sig_neg = jnp.where(pos, small, big)
        lf_ref[...] = jnp.log(jnp.where(pos, big, small) + lbf_ref[d:d + 1, :] * sig_neg)
        key_ref[...] = omlb_ref[d:d + 1, :] * sig_neg
    ia_ref[...] = seg(3).astype(ia_ref.dtype)
    ga_ref[...] = seg(4)
    qb = seg(5) * RB_DK ** -0.5
    kb = seg(6)
    if use_rope:
        qb = _rope_heads(qb, cos_ref[...], sin_ref[...], RB_HEADS)
        kb = _rope_heads(kb, cos_ref[...], sin_ref[...], RB_HEADS)
    qb_ref[...] = qb
    kb_ref[...] = kb
    vb_ref[...] = seg(7).astype(vb_ref.dtype)
    gb_ref[...] = seg(8)


def _row_tiled_call(body, name, x, sc, sh, consts, seq_inputs, out_widths, out_dtypes):
    b, l, d = x.shape
    tm = min(l, PROJ_ROWS)
    assert l % tm == 0

    def mod_spec(m):
        if m.shape[0] == b:
            return pl.BlockSpec((None, 1, d), lambda bi, ri: (bi, 0, 0))
        return pl.BlockSpec((None, 1, d), lambda bi, ri: (0, 0, 0))

    def const_spec(a):
        return pl.BlockSpec(a.shape, lambda bi, ri, nd=a.ndim: (0,) * nd)

    return pl.pallas_call(
        body,
        grid=(b, l // tm),
        in_specs=[pl.BlockSpec((None, tm, d), lambda bi, ri: (bi, ri, 0)), mod_spec(sc), mod_spec(sh)]
        + [const_spec(a) for a in consts]
        + [pl.BlockSpec((tm, a.shape[-1]), lambda bi, ri: (ri, 0)) for a in seq_inputs],
        out_specs=[pl.BlockSpec((None, tm, w), lambda bi, ri: (bi, ri, 0)) for w in out_widths],
        out_shape=[jax.ShapeDtypeStruct((b, l, w), dt) for w, dt in zip(out_widths, out_dtypes)],
        compiler_params=pltpu.CompilerParams(
            dimension_semantics=("parallel", "parallel"),
            vmem_limit_bytes=VMEM_LIMIT_BYTES),
        name=name,
    )(x, sc, sh, *consts, *seq_inputs)


def _even_mixer(x, sc, sh, w_in, lb, ha_norm_w, rb_log_gamma, rope, states):
    f32, bf = jnp.float32, jnp.bfloat16
    consts = [w_in.astype(bf), jnp.maximum(lb, LB_FLOOR), 1.0 - lb]
    seq = []
    if rope is not None:
        cos, sin = rope
        seq = [jnp.concatenate([cos, cos], axis=-1), jnp.concatenate([-sin, sin], axis=-1)]
    qa, ka_f, ka_b, lf_f, lf_b, ia, ga, qb, kb, vb, gb = _row_tiled_call(
        functools.partial(_even_in_body, use_rope=rope is not None), "even_in_proj", x, sc, sh, consts, seq,
        (HA_KEY,) * 5 + (HA_VAL, HA_VAL, RB_KEY, RB_KEY, RB_VAL, RB_VAL),
        (f32,) * 5 + (bf, f32, f32, f32, bf, f32))
    ya, sa_f, sa_b = _bidir_scan(qa, ka_f, ka_b, ia, lf_f, lf_b, ga, ha_norm_w, states[0][0], states[0][1],
                                 HA_HEADS)
    yb, sb_f, sb_b = _bidir_scan(qb, kb, kb, vb, None, None, gb, jnp.ones((RB_DV,), f32),
                                 states[1][0], states[1][1], RB_HEADS, log_decay=rb_log_gamma)
    return (ya, yb), ((sa_f, sa_b), (sb_f, sb_b))


def _odd_in_body(x_ref, sc_ref, sh_ref, w_ref, w2_ref, b2_ref, q_ref, k_ref, v_ref, g_ref, lff_ref, lfb_ref):
    h = _modulated_bf16(x_ref, sc_ref, sh_ref)
    offs = np.cumsum((0,) + ODD_SIZES[:4]).tolist()

    def seg(lo, hi):
        return jnp.dot(h, w_ref[:, lo:hi], preferred_element_type=jnp.float32)

    q_ref[...] = seg(offs[0], offs[1]) * GC_DK ** -0.5
    k_ref[...] = seg(offs[1], offs[2])
    v_ref[...] = seg(offs[2], offs[3]).astype(v_ref.dtype)
    g_ref[...] = seg(offs[3], offs[4])
    lr = seg(offs[4], w_ref.shape[-1]).astype(jnp.bfloat16)
    for d, lf_ref in enumerate((lff_ref, lfb_ref)):
        zz = jnp.dot(lr, w2_ref[d], preferred_element_type=jnp.float32) + b2_ref[d:d + 1, :]
        lf_ref[...] = _log_sigmoid(zz) / GC_TAU


def _odd_mixer(x, sc, sh, w_in, w2, b2, norm_w, states):
    f32, bf = jnp.float32, jnp.bfloat16
    lane = 128
    pad = lane - 2 * GC_RANK
    w_pad = jnp.pad(w_in, ((0, 0), (0, pad))).astype(bf)
    w2_pad = jnp.stack([jnp.pad(w2[0], ((0, lane - GC_RANK), (0, 0))),
                        jnp.pad(w2[1], ((GC_RANK, lane - 2 * GC_RANK), (0, 0)))]).astype(bf)
    q, k, v, g, lf_f, lf_b = _row_tiled_call(
        _odd_in_body, "odd_in_proj", x, sc, sh, [w_pad, w2_pad, b2], [],
        (GC_KEY, GC_KEY, GC_VAL, GC_VAL, GC_KEY, GC_KEY), (f32, f32, bf, f32, f32, f32))
    y, s_f, s_b = _bidir_scan(q, k, k, v, lf_f, lf_b, g, norm_w, states[0], states[1], GC_HEADS)
    return (y,), (s_f, s_b)


def _mixer_out_body(*refs, n_y):
    y_refs = refs[:n_y]
    (wo_ref, x_ref, g1_ref, lnw_ref, lnb_ref, sc2_ref, sh2_ref, wr_ref, x1_ref, h2_ref, aff_ref) = refs[n_y:]
    mix = None
    lo = 0
    for y_ref in y_refs:
        hi = lo + y_ref.shape[-1]
        part = jnp.dot(y_ref[...].astype(jnp.bfloat16), wo_ref[lo:hi, :], preferred_element_type=jnp.float32)
        mix = part if mix is None else mix + part
        lo = hi
    z = DN_ALPHA * x_ref[...] + g1_ref[...] * mix
    mu = jnp.mean(z, axis=-1, keepdims=True)
    zc = z - mu
    var = jnp.mean(zc * zc, axis=-1, keepdims=True)
    x1 = zc * lax.rsqrt(var + LN_EPS) * lnw_ref[...] + lnb_ref[...]
    x1_ref[...] = x1
    h2 = (x1 * (1.0 + sc2_ref[...]) + sh2_ref[...]).astype(jnp.bfloat16)
    h2_ref[...] = h2
    logits = lax.dot_general(wr_ref[...], h2, _NT, preferred_element_type=jnp.float32)
    ex = jnp.exp(logits - jnp.max(logits, axis=0, keepdims=True))
    aff_ref[...] = ex / jnp.sum(ex, axis=0, keepdims=True)


def _mixer_out(ys, w_out, x, g1, ln_w, ln_b, sc2, sh2, w_router):
    b, l, d = x.shape
    e = w_router.shape[-1]
    tm = min(l, PROJ_ROWS)
    assert l % tm == 0

    def mod_spec(m):
        if m.shape[0] == b:
            return pl.BlockSpec((None, 1, d), lambda bi, ri: (bi, 0, 0))
        return pl.BlockSpec((None, 1, d), lambda bi, ri: (0, 0, 0))

    def const_spec(a):
        return pl.BlockSpec(a.shape, lambda bi, ri, nd=a.ndim: (0,) * nd)

    def rows(w):
        return pl.BlockSpec((None, tm, w), lambda bi, ri: (bi, ri, 0))

    consts_a = [w_out.astype(jnp.bfloat16)]
    consts_b = [ln_w.reshape(1, d), ln_b.reshape(1, d)]
    wr_t = w_router.T.astype(jnp.bfloat16)
    return pl.pallas_call(
        functools.partial(_mixer_out_body, n_y=len(ys)),
        grid=(b, l // tm),
        in_specs=[rows(y.shape[-1]) for y in ys] + [const_spec(consts_a[0]), rows(d), mod_spec(g1)]
        + [const_spec(a) for a in consts_b] + [mod_spec(sc2), mod_spec(sh2), const_spec(wr_t)],
        out_specs=[rows(d), rows(d), pl.BlockSpec((None, e, tm), lambda bi, ri: (bi, 0, ri))],
        out_shape=[jax.ShapeDtypeStruct((b, l, d), jnp.float32), jax.ShapeDtypeStruct((b, l, d), jnp.bfloat16),
                   jax.ShapeDtypeStruct((b, e, l), jnp.float32)],
        compiler_params=pltpu.CompilerParams(
            dimension_semantics=("parallel", "parallel"),
            vmem_limit_bytes=VMEM_LIMIT_BYTES),
        name="mixer_out",
    )(*ys, *consts_a, x, g1, *consts_b, sc2, sh2, wr_t)


def _expert_ffn_body(x_ref, gate_ref, wg_ref, wu_ref, wd_ref, out_ref, hid_ref):
    tb, cap, d = x_ref.shape
    ff = wg_ref.shape[-1]
    x = x_ref[...].reshape(tb * cap, d)
    for j in range(ff // FFN_FF_TILE):
        cols = slice(j * FFN_FF_TILE, (j + 1) * FFN_FF_TILE)
        g = jnp.dot(x, wg_ref[:, cols], preferred_element_type=jnp.float32)
        u = jnp.dot(x, wu_ref[:, cols], preferred_element_type=jnp.float32)
        hid_ref[:, cols] = (g * jax.nn.sigmoid(g) * u).astype(jnp.bfloat16)
    y = jnp.dot(hid_ref[...], wd_ref[...], preferred_element_type=jnp.float32)
    out_ref[...] = (y.reshape(tb, cap, d) * gate_ref[...]).astype(out_ref.dtype)


def _expert_ffn(xs, gate, layer, w_gate, w_up, w_down):
    b, e, cap, d = xs.shape
    ff = w_gate.shape[-1]
    tb = max(1, min(b, FFN_ROWS_PER_STEP // cap))
    assert b % tb == 0 and ff % FFN_FF_TILE == 0
    return pl.pallas_call(
        _expert_ffn_body,
        grid=(e, b // tb),
        in_specs=[
            pl.BlockSpec((tb, None, cap, d), lambda ei, bi: (bi, ei, 0, 0)),
            pl.BlockSpec((tb, None, cap, 1), lambda ei, bi: (bi, ei, 0, 0)),
            pl.BlockSpec((None, None, d, ff), lambda ei, bi: (layer, ei, 0, 0)),
            pl.BlockSpec((None, None, d, ff), lambda ei, bi: (layer, ei, 0, 0)),
            pl.BlockSpec((None, None, ff, d), lambda ei, bi: (layer, ei, 0, 0)),
        ],
        out_specs=pl.BlockSpec((tb, None, cap, d), lambda ei, bi: (bi, ei, 0, 0)),
        out_shape=jax.ShapeDtypeStruct((b, e, cap, d), jnp.bfloat16),
        scratch_shapes=[pltpu.VMEM((tb * cap, ff), jnp.bfloat16)],
        compiler_params=pltpu.CompilerParams(
            dimension_semantics=("parallel", "parallel"),
            vmem_limit_bytes=VMEM_LIMIT_BYTES),
        name="expert_ffn",
    )(xs, gate, w_gate, w_up, w_down)


def _combine_ln_body(idx_ref, ys_ref, x_ref, g_ref, w_ref, b_ref, out_ref):
    tr, d = x_ref.shape
    e, cap, _ = ys_ref.shape
    r0 = pl.program_id(1) * tr
    tokens = lax.broadcasted_iota(jnp.int32, (tr, e * cap), 0) + r0
    onehot = jnp.where(tokens == idx_ref[...], 1.0, 0.0).astype(jnp.bfloat16)
    ffn = jnp.dot(onehot, ys_ref[...].reshape(e * cap, d), preferred_element_type=jnp.float32)
    z = DN_ALPHA * x_ref[...] + g_ref[...] * ffn
    mu = jnp.mean(z, axis=-1, keepdims=True)
    zc = z - mu
    var = jnp.mean(zc * zc, axis=-1, keepdims=True)
    out_ref[...] = zc * lax.rsqrt(var + LN_EPS) * w_ref[...] + b_ref[...]


def _combine_ln(ys, idx, x, g, w, b):
    bsz, e, cap, d = ys.shape
    n = x.shape[1]
    tr = min(n, COMBINE_ROWS)
    assert n % tr == 0
    per_sample = g.shape[0] == bsz
    return pl.pallas_call(
        _combine_ln_body,
        grid=(bsz, n // tr),
        in_specs=[
            pl.BlockSpec((None, 1, e * cap), lambda bi, ri: (bi, 0, 0)),
            pl.BlockSpec((None, e, cap, d), lambda bi, ri: (bi, 0, 0, 0)),
            pl.BlockSpec((None, tr, d), lambda bi, ri: (bi, ri, 0)),
            pl.BlockSpec((None, 1, d), (lambda bi, ri: (bi, 0, 0)) if per_sample else (lambda bi, ri: (0, 0, 0))),
            pl.BlockSpec((1, d), lambda bi, ri: (0, 0)),
            pl.BlockSpec((1, d), lambda bi, ri: (0, 0)),
        ],
        out_specs=pl.BlockSpec((None, tr, d), lambda bi, ri: (bi, ri, 0)),
        out_shape=jax.ShapeDtypeStruct((bsz, n, d), jnp.float32),
        compiler_params=pltpu.CompilerParams(
            dimension_semantics=("parallel", "parallel"),
            vmem_limit_bytes=VMEM_LIMIT_BYTES),
        name="combine_ln",
    )(idx.reshape(bsz, 1, e * cap), ys, x, g, w.reshape(1, d), b.reshape(1, d))


def _ec_ffn_ln(x, h, aff, g, layer, w_gate, w_up, w_down, ln_w, ln_b):
    n = x.shape[1]
    cap = EC_CAPACITY_FACTOR * n // N_EXPERTS
    gate, idx = lax.top_k(aff, cap)
    xs = jax.vmap(lambda hb, ib: hb[ib])(h, idx)
    ys = _expert_ffn(xs, gate[..., None], layer, w_gate, w_up, w_down)
    return _combine_ln(ys, idx, x, g, ln_w, ln_b)


def kernel(x, c, ctx, c_ctx, ada_w, ada_b, ln_w, ln_b, even_w_in, even_w_out, ha_lb, ha_norm,
           rb_decay, odd_w_in, odd_w_out, gc_w2, gc_b2, gc_norm, router_w, exp_w_gate, exp_w_up,
           exp_w_down):
    n_lat = x.shape[1]
    rows = n_lat // GRID_W
    rope = _axial_rope(rows)
    b_ctx = ctx.shape[0]
    lb_p = jax.nn.softmax(ha_lb, axis=0)
    lb_all = jnp.cumsum(lb_p, axis=0) - lb_p[0]
    cond_lat = jax.nn.silu(c)
    cond_ctx = jax.nn.silu(c_ctx)
    w_gate_bf, w_up_bf, w_down_bf = (w.astype(jnp.bfloat16) for w in (exp_w_gate, exp_w_up, exp_w_down))

    for l in range(DEPTH):
        last = l == DEPTH - 1
        mod_lat = (cond_lat @ ada_w[l] + ada_b[l])[:, None, :]
        mod_ctx = (cond_ctx @ ada_w[l] + ada_b[l])[None, None, :]
        sh1, sc1, g1, sh2, sc2, g2 = jnp.split(mod_lat, 6, axis=-1)
        csh1, csc1, cg1, csh2, csc2, cg2 = jnp.split(mod_ctx, 6, axis=-1)

        j = l // 2
        if l % 2 == 0:
            z_a = jnp.zeros((b_ctx, HA_HEADS, HA_DV, HA_DK), jnp.float32)
            z_b = jnp.zeros((b_ctx, RB_HEADS, RB_DV, RB_DK), jnp.float32)
            log_gamma = jax.nn.log_sigmoid(rb_decay[j])
            w_out = even_w_out[j]
            y_ctx, st = _even_mixer(ctx, csc1, csh1, even_w_in[j], lb_all[j], ha_norm[j], log_gamma, None,
                                    ((z_a, z_a), (z_b, z_b)))
            y_lat, _ = _even_mixer(x, sc1, sh1, even_w_in[j], lb_all[j], ha_norm[j], log_gamma, rope, st)
        else:
            z_c = jnp.zeros((b_ctx, GC_HEADS, GC_DV, GC_DK), jnp.float32)
            w_out = odd_w_out[j]
            y_ctx, st = _odd_mixer(ctx, csc1, csh1, odd_w_in[j], gc_w2[j], gc_b2[j], gc_norm[j], (z_c, z_c))
            y_lat, _ = _odd_mixer(x, sc1, sh1, odd_w_in[j], gc_w2[j], gc_b2[j], gc_norm[j], st)

        experts = (l, w_gate_bf, w_up_bf, w_down_bf)
        x, h2, aff = _mixer_out(y_lat, w_out, x, g1, ln_w[l, 0], ln_b[l, 0], sc2, sh2, router_w[l])
        x = _ec_ffn_ln(x, h2, aff, g2, *experts, ln_w[l, 1], ln_b[l, 1])
        if not last:
            ctx, h2, aff = _mixer_out(y_ctx, w_out, ctx, cg1, ln_w[l, 0], ln_b[l, 0], csc2, csh2, router_w[l])
            ctx = _ec_ffn_ln(ctx, h2, aff, cg2, *experts, ln_w[l, 1], ln_b[l, 1])
    return x
```

```python
import functools

import jax
import jax.numpy as jnp
import numpy as np
from jax import lax
from jax.experimental import pallas as pl
from jax.experimental.pallas import tpu as pltpu

D_MODEL = 1024
DEPTH = 4
GRID_W = 64

HA_HEADS = 4
HA_DK = 128
HA_DV = 128
RB_HEADS = 4
RB_DK = 128
RB_DV = 128
GC_HEADS = 4
GC_DK = 128
GC_DV = 256
GC_RANK = 16
GC_TAU = 16.0
N_EXPERTS = 16
EXPERT_FF = 2816
EC_CAPACITY_FACTOR = 2

CHUNK = 64
SUB = 16
N_SUB = CHUNK // SUB
ROPE_BASE = 10000.0
LN_EPS = 1e-5
RMS_EPS = 1e-6
LB_FLOOR = 1e-30
DN_ALPHA = (2 * DEPTH) ** 0.25

HA_KEY = HA_HEADS * HA_DK
HA_VAL = HA_HEADS * HA_DV
RB_KEY = RB_HEADS * RB_DK
RB_VAL = RB_HEADS * RB_DV
GC_KEY = GC_HEADS * GC_DK
GC_VAL = GC_HEADS * GC_DV
EVEN_SIZES = (HA_KEY, HA_KEY, HA_KEY, HA_VAL, HA_VAL, RB_KEY, RB_KEY, RB_VAL, RB_VAL)
ODD_SIZES = (GC_KEY, GC_KEY, GC_VAL, GC_VAL, 2 * GC_RANK)

VMEM_LIMIT_BYTES = 56 * 1024 * 1024
FFN_ROWS_PER_STEP = 1024
FFN_FF_TILE = 256
SCAN_EPILOGUE_ROWS = 256
COMBINE_ROWS = 256
PROJ_ROWS = 512
SCAN_GROUP = 8

_NT = (((1,), (1,)), ((), ()))
_TN = (((0,), (0,)), ((), ()))


def _axial_rope(rows):
    r_idx, c_idx = jnp.meshgrid(jnp.arange(rows), jnp.arange(GRID_W), indexing='ij')
    n_freq = RB_DK // 4
    freq = ROPE_BASE ** (-jnp.arange(n_freq, dtype=jnp.float32) / n_freq)
    ang = jnp.concatenate([r_idx.reshape(-1, 1).astype(jnp.float32) * freq,
                           c_idx.reshape(-1, 1).astype(jnp.float32) * freq], axis=-1)
    return jnp.cos(ang), jnp.sin(ang)


def _chunk_running_sum(lf, tri3):
    bf = jnp.bfloat16
    hi = lf.astype(bf)
    r1 = lf - hi.astype(jnp.float32)
    mid = r1.astype(bf)
    lo = (r1 - mid.astype(jnp.float32)).astype(bf)
    return jnp.dot(tri3, jnp.concatenate([hi, mid, lo], axis=0), preferred_element_type=jnp.float32)


def _chunk_operands(q, k, cs, rev):
    bf = jnp.bfloat16
    dk = q.shape[-1]
    zero_row = jnp.zeros((1, dk), jnp.float32)
    if not rev:
        ends = [cs[SUB * a + SUB - 1:SUB * a + SUB, :] for a in range(N_SUB)]
        starts = [zero_row] + ends[:-1]
        mids = [cs[SUB * a + SUB // 2 - 1:SUB * a + SUB // 2, :] for a in range(N_SUB)]
        order = list(range(N_SUB))
    else:
        ends = [cs[SUB * a:SUB * a + 1, :] for a in range(N_SUB)]
        starts = ends[1:] + [zero_row]
        mids = [cs[SUB * a + SUB // 2:SUB * a + SUB // 2 + 1, :] for a in range(N_SUB)]
        order = list(range(N_SUB - 1, -1, -1))
    last = ends[order[-1]]

    def per_block(rows):
        return jnp.concatenate([jnp.broadcast_to(r, (SUB, dk)) for r in rows], axis=0)

    mid_b, start_b, end_b = per_block(mids), per_block(starts), per_block(ends)
    qd = q * jnp.exp(cs - mid_b)
    kd = k * jnp.exp(mid_b - cs)
    q_in = q * jnp.exp(cs - start_b)
    k_out = k * jnp.exp(end_b - cs)
    q_state = q_in * jnp.exp(start_b)
    k_state = k_out * jnp.exp(last - end_b)

    def blk(t, a):
        return t[SUB * a:SUB * (a + 1), :]

    zeros_blk = jnp.zeros((SUB, dk), jnp.float32)
    q_cat, k_cat = [], []
    for s in range(N_SUB - 1):
        src = order[s]
        q_rows = [zeros_blk] * N_SUB
        k_rows = [zeros_blk] * N_SUB
        k_rows[src] = blk(k_out, src)
        for t in range(s + 1, N_SUB):
            tgt = order[t]
            piece = blk(q_in, tgt)
            if t > s + 1:
                piece = piece * jnp.exp(jnp.minimum(starts[tgt] - ends[src], 0.0))
            q_rows[tgt] = piece
        q_cat.append(jnp.concatenate(q_rows, axis=0))
        k_cat.append(jnp.concatenate(k_rows, axis=0))
    return dict(qd=qd.astype(bf), kd=kd.astype(bf),
                q_cat=jnp.concatenate(q_cat, axis=1).astype(bf), k_cat=jnp.concatenate(k_cat, axis=1).astype(bf),
                q_state=q_state.astype(bf), k_state=k_state.astype(bf), decay=jnp.exp(last))


def _bidir_scan_body(*refs, n_chunks, group, heads, scalar_decay):
    if scalar_decay:
        (q_ref, kf_ref, kb_ref, v_ref, lg_ref, gate_ref, nw_ref, s0f_ref, s0b_ref,
         y_ref, sf_ref, sb_ref, o_ref) = refs
    else:
        (q_ref, kf_ref, kb_ref, v_ref, lff_ref, lfb_ref, gate_ref, nw_ref, s0f_ref, s0b_ref,
         y_ref, sf_ref, sb_ref, o_ref) = refs
    c = CHUNK
    bf = jnp.bfloat16
    dk = q_ref.shape[-1]
    dv = o_ref.shape[-1]
    row = lax.broadcasted_iota(jnp.int32, (c, c), 0)
    col = lax.broadcasted_iota(jnp.int32, (c, c), 1)
    same_blk = (row // SUB) == (col // SUB)
    tri = {False: (col <= row), True: (col >= row)}
    tri3 = {r: jnp.concatenate([t.astype(bf)] * 3, axis=1) for r, t in tri.items()}
    dmask = {r: same_blk & t for r, t in tri.items()}
    k_refs = {False: kf_ref, True: kb_ref}
    st_refs = {False: sf_ref, True: sb_ref}
    sf_ref[...] = s0f_ref[...]
    sb_ref[...] = s0b_ref[...]
    if scalar_decay:
        pos = lax.broadcasted_iota(jnp.int32, (c, dk), 0).astype(jnp.float32)
        dist = (row - col).astype(jnp.float32)
        pair_decay, q_edge, k_edge, chunk_decay = {}, {}, {}, {}
        for hd in range(heads):
            for rev in (False, True):
                lg = lg_ref[int(rev), hd, 0:1, :]
                to_entry = (c - pos) if rev else (pos + 1.0)
                to_exit = pos if rev else (c - 1.0 - pos)
                signed = -dist if rev else dist
                pair_decay[hd, rev] = jnp.where(tri[rev], jnp.exp(lg[:, :c] * jnp.maximum(signed, 0.0)), 0.0)
                q_edge[hd, rev] = jnp.exp(lg * to_entry)
                k_edge[hd, rev] = jnp.exp(lg * to_exit)
                chunk_decay[hd, rev] = jnp.exp(lg * float(c))

    def step(n, accumulate):
        units = []
        for hd in range(heads):
            for u in range(group):
                for rev in (False, True):
                    chunk = n * group + u
                    if rev:
                        chunk = n_chunks - 1 - chunk
                    units.append((hd, rev, pl.ds(pl.multiple_of(chunk * c, c), c)))
        vs = [v_ref[hd, rows, :].astype(bf) for hd, _, rows in units]
        if scalar_decay:
            qs = [q_ref[hd, rows, :] for hd, _, rows in units]
            ks = [k_refs[rev][hd, rows, :] for hd, rev, rows in units]
            ops = [dict(q_state=(q * q_edge[hd, rev]).astype(bf), k_state=(k * k_edge[hd, rev]).astype(bf),
                        decay=chunk_decay[hd, rev]) for q, k, (hd, rev, _) in zip(qs, ks, units)]
            qk = [lax.dot_general(q.astype(bf), k.astype(bf), _NT, preferred_element_type=jnp.float32)
                  for q, k in zip(qs, ks)]
            scores = [(s * pair_decay[hd, rev]).astype(bf) for s, (hd, rev, _) in zip(qk, units)]
        else:
            lfs = [(lfb_ref if rev else lff_ref)[hd, rows, :] for hd, rev, rows in units]
            css = [_chunk_running_sum(lf, tri3[rev]) for lf, (_, rev, _) in zip(lfs, units)]
            ops = [_chunk_operands(q_ref[hd, rows, :], k_refs[rev][hd, rows, :], cs, rev)
                   for cs, (hd, rev, rows) in zip(css, units)]
            diag = [lax.dot_general(p["qd"], p["kd"], _NT, preferred_element_type=jnp.float32) for p in ops]
            cross = [lax.dot_general(p["q_cat"], p["k_cat"], _NT, preferred_element_type=jnp.float32) for p in ops]
            scores = [(jnp.where(dmask[rev], d, 0.0) + x).astype(bf)
                      for d, x, (_, rev, _) in zip(diag, cross, units)]
        kv = [lax.dot_general(v, p["k_state"], _TN, preferred_element_type=jnp.float32) for v, p in zip(vs, ops)]
        intra = [jnp.dot(s, v, preferred_element_type=jnp.float32) for s, v in zip(scores, vs)]
        for hd in range(heads):
            for rev in (False, True):
                st = st_refs[rev][hd]
                for i, (h_i, r, rows) in enumerate(units):
                    if (h_i, r) != (hd, rev):
                        continue
                    o = intra[i] + lax.dot_general(ops[i]["q_state"], st.astype(bf), _NT,
                                                   preferred_element_type=jnp.float32)
                    st = st * ops[i]["decay"] + kv[i]
                    if accumulate:
                        o_ref[hd, rows, :] += o
                    else:
                        o_ref[hd, rows, :] = o
                st_refs[rev][hd] = st

    steps = n_chunks // group

    def first(n, carry):
        step(n, False)
        return carry

    def second(n, carry):
        step(n, True)
        return carry

    lax.fori_loop(0, steps // 2, first, 0)
    lax.fori_loop(steps // 2, steps, second, 0)

    tr = min(SCAN_EPILOGUE_ROWS, n_chunks * c)

    def epilogue(i, carry):
        rows = pl.ds(pl.multiple_of(i * tr, tr), tr)
        for hd in range(heads):
            o = o_ref[hd, rows, :]
            g = gate_ref[hd, rows, :]
            y = o * lax.rsqrt(jnp.mean(o * o, axis=-1, keepdims=True) + RMS_EPS) * nw_ref[...]
            y_ref[rows, hd * dv:(hd + 1) * dv] = (y * (g * jax.nn.sigmoid(g))).astype(y_ref.dtype)
        return carry

    lax.fori_loop(0, n_chunks * c // tr, epilogue, 0)


def _bidir_scan(q, k_f, k_b, v, lf_f, lf_b, gate, norm_w, s0_f, s0_b, n_heads, log_decay=None):
    b, _, l, dk = q.shape
    dv = v.shape[-1]
    n_chunks = l // CHUNK
    group = min(SCAN_GROUP, n_chunks // 2)
    heads = min(n_heads, SCAN_GROUP // group)
    assert l % CHUNK == 0 and n_chunks % (2 * group) == 0 and n_heads % heads == 0
    scalar_decay = log_decay is not None
    seq_k = pl.BlockSpec((None, heads, l, dk), lambda bi, hi: (bi, hi, 0, 0))
    seq_v = pl.BlockSpec((None, heads, l, dv), lambda bi, hi: (bi, hi, 0, 0))
    out_y = pl.BlockSpec((None, l, heads * dv), lambda bi, hi: (bi, 0, hi))
    state = pl.BlockSpec((None, heads, dv, dk), lambda bi, hi: (bi, hi, 0, 0))
    if scalar_decay:
        lg = jnp.broadcast_to(log_decay.astype(jnp.float32)[:, :, None, None], (2, n_heads, 8, dk))
        decay_args = (lg,)
        decay_specs = [pl.BlockSpec((2, heads, 8, dk), lambda bi, hi: (0, hi, 0, 0))]
    else:
        decay_args = (lf_f, lf_b)
        decay_specs = [seq_k, seq_k]
    return pl.pallas_call(
        functools.partial(_bidir_scan_body, n_chunks=n_chunks, group=group, heads=heads,
                          scalar_decay=scalar_decay),
        grid=(b, n_heads // heads),
        in_specs=[seq_k, seq_k, seq_k, seq_v] + decay_specs + [
            seq_v, pl.BlockSpec((1, dv), lambda bi, hi: (0, 0)), state, state],
        out_specs=[out_y, state, state],
        out_shape=[jax.ShapeDtypeStruct((b, l, n_heads * dv), jnp.bfloat16),
                   jax.ShapeDtypeStruct((b, n_heads, dv, dk), jnp.float32),
                   jax.ShapeDtypeStruct((b, n_heads, dv, dk), jnp.float32)],
        scratch_shapes=[pltpu.VMEM((heads, l, dv), jnp.float32)],
        compiler_params=pltpu.CompilerParams(
            dimension_semantics=("parallel", "parallel"),
            vmem_limit_bytes=VMEM_LIMIT_BYTES),
        name="bidir_scan",
    )(q, k_f, k_b, v, *decay_args, gate, norm_w.astype(jnp.float32).reshape(1, dv), s0_f, s0_b)


def _log_sigmoid(x):
    return jnp.minimum(x, 0.0) - jnp.log(1.0 + jnp.exp(-jnp.abs(x)))


def _store_heads(ref, t):
    n_heads, _, dh = ref.shape
    for hh in range(n_heads):
        ref[hh] = t[:, hh * dh:(hh + 1) * dh].astype(ref.dtype)


def _modulated_bf16(x_ref, sc_ref, sh_ref):
    return (x_ref[...] * (1.0 + sc_ref[...]) + sh_ref[...]).astype(jnp.bfloat16)


def _rope_heads(t, cos2, sin2, n_heads):
    dk = t.shape[-1] // n_heads
    out = []
    for hh in range(n_heads):
        th = t[:, hh * dk:(hh + 1) * dk]
        out.append(th * cos2 + pltpu.roll(th, dk // 2, axis=1) * sin2)
    return jnp.concatenate(out, axis=-1)


def _even_in_body(*refs, use_rope):
    if use_rope:
        x_ref, sc_ref, sh_ref, w_ref, lbf_ref, omlb_ref, cos_ref, sin_ref = refs[:8]
    else:
        x_ref, sc_ref, sh_ref, w_ref, lbf_ref, omlb_ref = refs[:6]
    (qa_ref, kaf_ref, kab_ref, lff_ref, lfb_ref, ia_ref, ga_ref, qb_ref, kb_ref, vb_ref, gb_ref) = refs[-11:]
    h = _modulated_bf16(x_ref, sc_ref, sh_ref)
    offs = np.cumsum((0,) + EVEN_SIZES).tolist()

    def seg(i):
        return jnp.dot(h, w_ref[:, offs[i]:offs[i + 1]], preferred_element_type=jnp.float32)

    _store_heads(qa_ref, seg(0) * HA_DK ** -0.5)
    for d, (lf_ref, key_ref) in enumerate(((lff_ref, kaf_ref), (lfb_ref, kab_ref))):
        z = seg(1 + d)
        e = jnp.exp(-jnp.abs(z))
        big = 1.0 / (1.0 + e)
        small = e * big
        pos = z >= 0.0
        sig_neg = jnp.where(pos, small, big)
        _store_heads(lf_ref, jnp.log(jnp.where(pos, big, small) + lbf_ref[d:d + 1, :] * sig_neg))
        _store_heads(key_ref, omlb_ref[d:d + 1, :] * sig_neg)
    _store_heads(ia_ref, seg(3))
    _store_heads(ga_ref, seg(4))
    qb = seg(5) * RB_DK ** -0.5
    kb = seg(6)
    if use_rope:
        qb = _rope_heads(qb, cos_ref[...], sin_ref[...], RB_HEADS)
        kb = _rope_heads(kb, cos_ref[...], sin_ref[...], RB_HEADS)
    _store_heads(qb_ref, qb)
    _store_heads(kb_ref, kb)
    _store_heads(vb_ref, seg(7))
    _store_heads(gb_ref, seg(8))


def _row_tiled_call(body, name, x, sc, sh, consts, seq_inputs, n_heads, out_widths, out_dtypes):
    b, l, d = x.shape
    tm = min(l, PROJ_ROWS)
    assert l % tm == 0

    def mod_spec(m):
        if m.shape[0] == b:
            return pl.BlockSpec((None, 1, d), lambda bi, ri: (bi, 0, 0))
        return pl.BlockSpec((None, 1, d), lambda bi, ri: (0, 0, 0))

    def const_spec(a):
        return pl.BlockSpec(a.shape, lambda bi, ri, nd=a.ndim: (0,) * nd)

    return pl.pallas_call(
        body,
        grid=(b, l // tm),
        in_specs=[pl.BlockSpec((None, tm, d), lambda bi, ri: (bi, ri, 0)), mod_spec(sc), mod_spec(sh)]
        + [const_spec(a) for a in consts]
        + [pl.BlockSpec((tm, a.shape[-1]), lambda bi, ri: (ri, 0)) for a in seq_inputs],
        out_specs=[pl.BlockSpec((None, n_heads, tm, w // n_heads), lambda bi, ri: (bi, 0, ri, 0))
                   for w in out_widths],
        out_shape=[jax.ShapeDtypeStruct((b, n_heads, l, w // n_heads), dt)
                   for w, dt in zip(out_widths, out_dtypes)],
        compiler_params=pltpu.CompilerParams(
            dimension_semantics=("parallel", "parallel"),
            vmem_limit_bytes=VMEM_LIMIT_BYTES),
        name=name,
    )(x, sc, sh, *consts, *seq_inputs)


def _even_mixer(x, sc, sh, w_in, lb, ha_norm_w, rb_log_gamma, rope, states):
    f32, bf = jnp.float32, jnp.bfloat16
    assert HA_HEADS == RB_HEADS
    consts = [w_in.astype(bf), jnp.maximum(lb, LB_FLOOR), 1.0 - lb]
    seq = []
    if rope is not None:
        cos, sin = rope
        seq = [jnp.concatenate([cos, cos], axis=-1), jnp.concatenate([-sin, sin], axis=-1)]
    qa, ka_f, ka_b, lf_f, lf_b, ia, ga, qb, kb, vb, gb = _row_tiled_call(
        functools.partial(_even_in_body, use_rope=rope is not None), "even_in_proj", x, sc, sh, consts, seq,
        HA_HEADS, (HA_KEY,) * 5 + (HA_VAL, HA_VAL, RB_KEY, RB_KEY, RB_VAL, RB_VAL),
        (f32,) * 5 + (bf, f32, f32, f32, bf, f32))
    ya, sa_f, sa_b = _bidir_scan(qa, ka_f, ka_b, ia, lf_f, lf_b, ga, ha_norm_w, states[0][0], states[0][1],
                                 HA_HEADS)
    yb, sb_f, sb_b = _bidir_scan(qb, kb, kb, vb, None, None, gb, jnp.ones((RB_DV,), f32),
                                 states[1][0], states[1][1], RB_HEADS, log_decay=rb_log_gamma)
    return (ya, yb), ((sa_f, sa_b), (sb_f, sb_b))


def _odd_in_body(x_ref, sc_ref, sh_ref, w_ref, w2_ref, b2_ref, q_ref, k_ref, v_ref, g_ref, lff_ref, lfb_ref):
    h = _modulated_bf16(x_ref, sc_ref, sh_ref)
    offs = np.cumsum((0,) + ODD_SIZES[:4]).tolist()

    def seg(lo, hi):
        return jnp.dot(h, w_ref[:, lo:hi], preferred_element_type=jnp.float32)

    _store_heads(q_ref, seg(offs[0], offs[1]) * GC_DK ** -0.5)
    _store_heads(k_ref, seg(offs[1], offs[2]))
    _store_heads(v_ref, seg(offs[2], offs[3]))
    _store_heads(g_ref, seg(offs[3], offs[4]))
    lr = seg(offs[4], w_ref.shape[-1]).astype(jnp.bfloat16)
    for d, lf_ref in enumerate((lff_ref, lfb_ref)):
        zz = jnp.dot(lr, w2_ref[d], preferred_element_type=jnp.float32) + b2_ref[d:d + 1, :]
        _store_heads(lf_ref, _log_sigmoid(zz) / GC_TAU)


def _odd_mixer(x, sc, sh, w_in, w2, b2, norm_w, states):
    f32, bf = jnp.float32, jnp.bfloat16
    lane = 128
    pad = lane - 2 * GC_RANK
    w_pad = jnp.pad(w_in, ((0, 0), (0, pad))).astype(bf)
    w2_pad = jnp.stack([jnp.pad(w2[0], ((0, lane - GC_RANK), (0, 0))),
                        jnp.pad(w2[1], ((GC_RANK, lane - 2 * GC_RANK), (0, 0)))]).astype(bf)
    q, k, v, g, lf_f, lf_b = _row_tiled_call(
        _odd_in_body, "odd_in_proj", x, sc, sh, [w_pad, w2_pad, b2], [], GC_HEADS,
        (GC_KEY, GC_KEY, GC_VAL, GC_VAL, GC_KEY, GC_KEY), (f32, f32, bf, f32, f32, f32))
    y, s_f, s_b = _bidir_scan(q, k, k, v, lf_f, lf_b, g, norm_w, states[0], states[1], GC_HEADS)
    return (y,), (s_f, s_b)


def _mixer_out_body(*refs, n_y):
    y_refs = refs[:n_y]
    (wo_ref, x_ref, g1_ref, lnw_ref, lnb_ref, sc2_ref, sh2_ref, wr_ref, x1_ref, h2_ref, aff_ref) = refs[n_y:]
    mix = None
    lo = 0
    for y_ref in y_refs:
        hi = lo + y_ref.shape[-1]
        part = jnp.dot(y_ref[...], wo_ref[lo:hi, :], preferred_element_type=jnp.float32)
        mix = part if mix is None else mix + part
        lo = hi
    z = DN_ALPHA * x_ref[...] + g1_ref[...] * mix
    mu = jnp.mean(z, axis=-1, keepdims=True)
    zc = z - mu
    var = jnp.mean(zc * zc, axis=-1, keepdims=True)
    x1 = zc * lax.rsqrt(var + LN_EPS) * lnw_ref[...] + lnb_ref[...]
    x1_ref[...] = x1
    h2 = (x1 * (1.0 + sc2_ref[...]) + sh2_ref[...]).astype(jnp.bfloat16)
    h2_ref[...] = h2
    logits = lax.dot_general(wr_ref[...], h2, _NT, preferred_element_type=jnp.float32)
    ex = jnp.exp(logits - jnp.max(logits, axis=0, keepdims=True))
    aff_ref[...] = ex / jnp.sum(ex, axis=0, keepdims=True)


def _mixer_out(ys, w_out, x, g1, ln_w, ln_b, sc2, sh2, w_router):
    b, l, d = x.shape
    e = w_router.shape[-1]
    tm = min(l, PROJ_ROWS)
    assert l % tm == 0

    def mod_spec(m):
        if m.shape[0] == b:
            return pl.BlockSpec((None, 1, d), lambda bi, ri: (bi, 0, 0))
        return pl.BlockSpec((None, 1, d), lambda bi, ri: (0, 0, 0))

    def const_spec(a):
        return pl.BlockSpec(a.shape, lambda bi, ri, nd=a.ndim: (0,) * nd)

    def rows(w):
        return pl.BlockSpec((None, tm, w), lambda bi, ri: (bi, ri, 0))

    consts_a = [w_out.astype(jnp.bfloat16)]
    consts_b = [ln_w.reshape(1, d), ln_b.reshape(1, d)]
    wr_t = w_router.T.astype(jnp.bfloat16)
    return pl.pallas_call(
        functools.partial(_mixer_out_body, n_y=len(ys)),
        grid=(b, l // tm),
        in_specs=[rows(y.shape[-1]) for y in ys] + [const_spec(consts_a[0]), rows(d), mod_spec(g1)]
        + [const_spec(a) for a in consts_b] + [mod_spec(sc2), mod_spec(sh2), const_spec(wr_t)],
        out_specs=[rows(d), rows(d), pl.BlockSpec((None, e, tm), lambda bi, ri: (bi, 0, ri))],
        out_shape=[jax.ShapeDtypeStruct((b, l, d), jnp.float32), jax.ShapeDtypeStruct((b, l, d), jnp.bfloat16),
                   jax.ShapeDtypeStruct((b, e, l), jnp.float32)],
        compiler_params=pltpu.CompilerParams(
            dimension_semantics=("parallel", "parallel"),
            vmem_limit_bytes=VMEM_LIMIT_BYTES),
        name="mixer_out",
    )(*ys, *consts_a, x, g1, *consts_b, sc2, sh2, wr_t)


def _expert_ffn_body(x_ref, gate_ref, wg_ref, wu_ref, wd_ref, out_ref, hid_ref):
    tb, cap, d = x_ref.shape
    ff = wg_ref.shape[-1]
    x = x_ref[...].reshape(tb * cap, d)
    for j in range(ff // FFN_FF_TILE):
        cols = slice(j * FFN_FF_TILE, (j + 1) * FFN_FF_TILE)
        g = jnp.dot(x, wg_ref[:, cols], preferred_element_type=jnp.float32)
        u = jnp.dot(x, wu_ref[:, cols], preferred_element_type=jnp.float32)
        hid_ref[:, cols] = (g * jax.nn.sigmoid(g) * u).astype(jnp.bfloat16)
    y = jnp.dot(hid_ref[...], wd_ref[...], preferred_element_type=jnp.float32)
    out_ref[...] = (y.reshape(tb, cap, d) * gate_ref[...]).astype(out_ref.dtype)


def _expert_ffn(xs, gate, layer, w_gate, w_up, w_down):
    b, e, cap, d = xs.shape
    ff = w_gate.shape[-1]
    tb = max(1, min(b, FFN_ROWS_PER_STEP // cap))
    assert b % tb == 0 and ff % FFN_FF_TILE == 0
    return pl.pallas_call(
        _expert_ffn_body,
        grid=(e, b // tb),
        in_specs=[
            pl.BlockSpec((tb, None, cap, d), lambda ei, bi: (bi, ei, 0, 0)),
            pl.BlockSpec((tb, None, cap, 1), lambda ei, bi: (bi, ei, 0, 0)),
            pl.BlockSpec((None, None, d, ff), lambda ei, bi: (layer, ei, 0, 0)),
            pl.BlockSpec((None, None, d, ff), lambda ei, bi: (layer, ei, 0, 0)),
            pl.BlockSpec((None, None, ff, d), lambda ei, bi: (layer, ei, 0, 0)),
        ],
        out_specs=pl.BlockSpec((tb, None, cap, d), lambda ei, bi: (bi, ei, 0, 0)),
        out_shape=jax.ShapeDtypeStruct((b, e, cap, d), jnp.bfloat16),
        scratch_shapes=[pltpu.VMEM((tb * cap, ff), jnp.bfloat16)],
        compiler_params=pltpu.CompilerParams(
            dimension_semantics=("parallel", "parallel"),
            vmem_limit_bytes=VMEM_LIMIT_BYTES),
        name="expert_ffn",
    )(xs, gate, w_gate, w_up, w_down)


def _combine_ln_body(idx_ref, ys_ref, x_ref, g_ref, w_ref, b_ref, out_ref):
    tr, d = x_ref.shape
    e, cap, _ = ys_ref.shape
    r0 = pl.program_id(1) * tr
    tokens = lax.broadcasted_iota(jnp.int32, (tr, e * cap), 0) + r0
    onehot = jnp.where(tokens == idx_ref[...], 1.0, 0.0).astype(jnp.bfloat16)
    ffn = jnp.dot(onehot, ys_ref[...].reshape(e * cap, d), preferred_element_type=jnp.float32)
    z = DN_ALPHA * x_ref[...] + g_ref[...] * ffn
    mu = jnp.mean(z, axis=-1, keepdims=True)
    zc = z - mu
    var = jnp.mean(zc * zc, axis=-1, keepdims=True)
    out_ref[...] = zc * lax.rsqrt(var + LN_EPS) * w_ref[...] + b_ref[...]


def _combine_ln(ys, idx, x, g, w, b):
    bsz, e, cap, d = ys.shape
    n = x.shape[1]
    tr = min(n, COMBINE_ROWS)
    assert n % tr == 0
    per_sample = g.shape[0] == bsz
    return pl.pallas_call(
        _combine_ln_body,
        grid=(bsz, n // tr),
        in_specs=[
            pl.BlockSpec((None, 1, e * cap), lambda bi, ri: (bi, 0, 0)),
            pl.BlockSpec((None, e, cap, d), lambda bi, ri: (bi, 0, 0, 0)),
            pl.BlockSpec((None, tr, d), lambda bi, ri: (bi, ri, 0)),
            pl.BlockSpec((None, 1, d), (lambda bi, ri: (bi, 0, 0)) if per_sample else (lambda bi, ri: (0, 0, 0))),
            pl.BlockSpec((1, d), lambda bi, ri: (0, 0)),
            pl.BlockSpec((1, d), lambda bi, ri: (0, 0)),
        ],
        out_specs=pl.BlockSpec((None, tr, d), lambda bi, ri: (bi, ri, 0)),
        out_shape=jax.ShapeDtypeStruct((bsz, n, d), jnp.float32),
        compiler_params=pltpu.CompilerParams(
            dimension_semantics=("parallel", "parallel"),
            vmem_limit_bytes=VMEM_LIMIT_BYTES),
        name="combine_ln",
    )(idx.reshape(bsz, 1, e * cap), ys, x, g, w.reshape(1, d), b.reshape(1, d))


def _ec_ffn_ln(x, h, aff, g, layer, w_gate, w_up, w_down, ln_w, ln_b):
    n = x.shape[1]
    cap = EC_CAPACITY_FACTOR * n // N_EXPERTS
    gate, idx = lax.top_k(aff, cap)
    xs = jax.vmap(lambda hb, ib: hb[ib])(h, idx)
    ys = _expert_ffn(xs, gate[..., None], layer, w_gate, w_up, w_down)
    return _combine_ln(ys, idx, x, g, ln_w, ln_b)


def kernel(x, c, ctx, c_ctx, ada_w, ada_b, ln_w, ln_b, even_w_in, even_w_out, ha_lb, ha_norm,
           rb_decay, odd_w_in, odd_w_out, gc_w2, gc_b2, gc_norm, router_w, exp_w_gate, exp_w_up,
           exp_w_down):
    n_lat = x.shape[1]
    rows = n_lat // GRID_W
    rope = _axial_rope(rows)
    b_ctx = ctx.shape[0]
    lb_p = jax.nn.softmax(ha_lb, axis=0)
    lb_all = jnp.cumsum(lb_p, axis=0) - lb_p[0]
    cond_lat = jax.nn.silu(c)
    cond_ctx = jax.nn.silu(c_ctx)
    w_gate_bf, w_up_bf, w_down_bf = (w.astype(jnp.bfloat16) for w in (exp_w_gate, exp_w_up, exp_w_down))

    for l in range(DEPTH):
        last = l == DEPTH - 1
        mod_lat = (cond_lat @ ada_w[l] + ada_b[l])[:, None, :]
        mod_ctx = (cond_ctx @ ada_w[l] + ada_b[l])[None, None, :]
        sh1, sc1, g1, sh2, sc2, g2 = jnp.split(mod_lat, 6, axis=-1)
        csh1, csc1, cg1, csh2, csc2, cg2 = jnp.split(mod_ctx, 6, axis=-1)

        j = l // 2
        if l % 2 == 0:
            z_a = jnp.zeros((b_ctx, HA_HEADS, HA_DV, HA_DK), jnp.float32)
            z_b = jnp.zeros((b_ctx, RB_HEADS, RB_DV, RB_DK), jnp.float32)
            log_gamma = jax.nn.log_sigmoid(rb_decay[j])
            w_out = even_w_out[j]
            y_ctx, st = _even_mixer(ctx, csc1, csh1, even_w_in[j], lb_all[j], ha_norm[j], log_gamma, None,
                                    ((z_a, z_a), (z_b, z_b)))
            y_lat, _ = _even_mixer(x, sc1, sh1, even_w_in[j], lb_all[j], ha_norm[j], log_gamma, rope, st)
        else:
            z_c = jnp.zeros((b_ctx, GC_HEADS, GC_DV, GC_DK), jnp.float32)
            w_out = odd_w_out[j]
            y_ctx, st = _odd_mixer(ctx, csc1, csh1, odd_w_in[j], gc_w2[j], gc_b2[j], gc_norm[j], (z_c, z_c))
            y_lat, _ = _odd_mixer(x, sc1, sh1, odd_w_in[j], gc_w2[j], gc_b2[j], gc_norm[j], st)

        experts = (l, w_gate_bf, w_up_bf, w_down_bf)
        x, h2, aff = _mixer_out(y_lat, w_out, x, g1, ln_w[l, 0], ln_b[l, 0], sc2, sh2, router_w[l])
        x = _ec_ffn_ln(x, h2, aff, g2, *experts, ln_w[l, 1], ln_b[l, 1])
        if not last:
            ctx, h2, aff = _mixer_out(y_ctx, w_out, ctx, cg1, ln_w[l, 0], ln_b[l, 0], csc2, csh2, router_w[l])
            ctx = _ec_ffn_ln(ctx, h2, aff, cg2, *experts, ln_w[l, 1], ln_b[l, 1])
    return x
```

```python
import functools

import jax
import jax.numpy as jnp
import numpy as np
from jax import lax
from jax.experimental import pallas as pl
from jax.experimental.pallas import tpu as pltpu

D_MODEL = 1024
DEPTH = 4
GRID_W = 64

HA_HEADS = 4
HA_DK = 128
HA_DV = 128
RB_HEADS = 4
RB_DK = 128
RB_DV = 128
GC_HEADS = 4
GC_DK = 128
GC_DV = 256
GC_RANK = 16
GC_TAU = 16.0
N_EXPERTS = 16
EXPERT_FF = 2816
EC_CAPACITY_FACTOR = 2

CHUNK = 64
SUB = 16
N_SUB = CHUNK // SUB
ROPE_BASE = 10000.0
LN_EPS = 1e-5
RMS_EPS = 1e-6
LB_FLOOR = 1e-30
DN_ALPHA = (2 * DEPTH) ** 0.25
LOG2_E = 1.4426950408889634

HA_KEY = HA_HEADS * HA_DK
HA_VAL = HA_HEADS * HA_DV
RB_KEY = RB_HEADS * RB_DK
RB_VAL = RB_HEADS * RB_DV
GC_KEY = GC_HEADS * GC_DK
GC_VAL = GC_HEADS * GC_DV
EVEN_SIZES = (HA_KEY, HA_KEY, HA_KEY, HA_VAL, HA_VAL, RB_KEY, RB_KEY, RB_VAL, RB_VAL)
ODD_SIZES = (GC_KEY, GC_KEY, GC_VAL, GC_VAL, 2 * GC_RANK)

VMEM_LIMIT_BYTES = 56 * 1024 * 1024
FFN_ROWS_PER_STEP = 1024
FFN_FF_TILE = 256
SCAN_EPILOGUE_ROWS = 256
COMBINE_ROWS = 256
COMBINE_WINDOW = 64
CONTRACT_DEPTH = 256
BF16_SUBLANES = 16
PROJ_ROWS = 512
MIXER_OUT_ROWS = 1024
SCAN_GROUP = 16

_NT = (((1,), (1,)), ((), ()))
_TN = (((0,), (0,)), ((), ()))


def _axial_rope(rows):
    r_idx, c_idx = jnp.meshgrid(jnp.arange(rows), jnp.arange(GRID_W), indexing='ij')
    n_freq = RB_DK // 4
    freq = ROPE_BASE ** (-jnp.arange(n_freq, dtype=jnp.float32) / n_freq)
    ang = jnp.concatenate([r_idx.reshape(-1, 1).astype(jnp.float32) * freq,
                           c_idx.reshape(-1, 1).astype(jnp.float32) * freq], axis=-1)
    return jnp.cos(ang), jnp.sin(ang)


def _chunk_running_sum(lf, rev):
    c, dk = lf.shape
    row = lax.broadcasted_iota(jnp.int32, (c, dk), 0)
    x = lf
    shift = 1
    while shift < c:
        if rev:
            x = x + jnp.where(row < c - shift, pltpu.roll(x, c - shift, axis=0), 0.0)
        else:
            x = x + jnp.where(row >= shift, pltpu.roll(x, shift, axis=0), 0.0)
        shift *= 2
    return x


def _chunk_operands(q, k, cs, rev):
    bf = jnp.bfloat16
    dk = q.shape[-1]
    cs = cs * LOG2_E
    zero_row = jnp.zeros((1, dk), jnp.float32)
    if not rev:
        ends = [cs[SUB * a + SUB - 1:SUB * a + SUB, :] for a in range(N_SUB)]
        starts = [zero_row] + ends[:-1]
        mids = [cs[SUB * a + SUB // 2 - 1:SUB * a + SUB // 2, :] for a in range(N_SUB)]
        order = list(range(N_SUB))
    else:
        ends = [cs[SUB * a:SUB * a + 1, :] for a in range(N_SUB)]
        starts = ends[1:] + [zero_row]
        mids = [cs[SUB * a + SUB // 2:SUB * a + SUB // 2 + 1, :] for a in range(N_SUB)]
        order = list(range(N_SUB - 1, -1, -1))
    last = ends[order[-1]]

    def per_block(rows):
        return jnp.concatenate([jnp.broadcast_to(r, (SUB, dk)) for r in rows], axis=0)

    mid_b, start_b, end_b = per_block(mids), per_block(starts), per_block(ends)
    qd = q * jnp.exp2(cs - mid_b)
    kd = k * jnp.exp2(mid_b - cs)
    q_in = q * jnp.exp2(cs - start_b)
    k_out = k * jnp.exp2(end_b - cs)
    q_state = q_in * jnp.exp2(start_b)
    k_state = k_out * jnp.exp2(last - end_b)

    def blk(t, a):
        return t[SUB * a:SUB * (a + 1), :]

    zeros_blk = jnp.zeros((SUB, dk), jnp.float32)
    q_cat, k_cat = [], []
    for s in range(N_SUB - 1):
        src = order[s]
        q_rows = [zeros_blk] * N_SUB
        k_rows = [zeros_blk] * N_SUB
        k_rows[src] = blk(k_out, src)
        for t in range(s + 1, N_SUB):
            tgt = order[t]
            piece = blk(q_in, tgt)
            if t > s + 1:
                piece = piece * jnp.exp2(jnp.minimum(starts[tgt] - ends[src], 0.0))
            q_rows[tgt] = piece
        q_cat.append(jnp.concatenate(q_rows, axis=0))
        k_cat.append(jnp.concatenate(k_rows, axis=0))
    return dict(qd=qd.astype(bf), kd=kd.astype(bf),
                q_cat=jnp.concatenate(q_cat, axis=1).astype(bf), k_cat=jnp.concatenate(k_cat, axis=1).astype(bf),
                q_state=q_state.astype(bf), k_state=k_state.astype(bf), decay=jnp.exp2(last))


def _bidir_scan_body(*refs, n_chunks, group, heads, scalar_decay):
    if scalar_decay:
        (q_ref, kf_ref, kb_ref, v_ref, lg_ref, gate_ref, nw_ref, s0f_ref, s0b_ref,
         y_ref, sf_ref, sb_ref, o_ref) = refs
    else:
        (q_ref, kf_ref, kb_ref, v_ref, lff_ref, lfb_ref, gate_ref, nw_ref, s0f_ref, s0b_ref,
         y_ref, sf_ref, sb_ref, o_ref) = refs
    c = CHUNK
    bf = jnp.bfloat16
    dk = q_ref.shape[-1]
    dv = o_ref.shape[-1]
    row = lax.broadcasted_iota(jnp.int32, (c, c), 0)
    col = lax.broadcasted_iota(jnp.int32, (c, c), 1)
    same_blk = (row // SUB) == (col // SUB)
    tri = {False: (col <= row), True: (col >= row)}
    dmask = {r: same_blk & t for r, t in tri.items()}
    k_refs = {False: kf_ref, True: kb_ref}
    st_refs = {False: sf_ref, True: sb_ref}
    sf_ref[...] = s0f_ref[...]
    sb_ref[...] = s0b_ref[...]
    if scalar_decay:
        pos = lax.broadcasted_iota(jnp.int32, (c, dk), 0).astype(jnp.float32)
        dist = (row - col).astype(jnp.float32)
        pair_decay, q_edge, k_edge, chunk_decay = {}, {}, {}, {}
        for hd in range(heads):
            for rev in (False, True):
                lg = lg_ref[int(rev), hd, 0:1, :]
                to_entry = (c - pos) if rev else (pos + 1.0)
                to_exit = pos if rev else (c - 1.0 - pos)
                signed = -dist if rev else dist
                pair_decay[hd, rev] = jnp.where(tri[rev], jnp.exp(lg[:, :c] * jnp.maximum(signed, 0.0)), 0.0)
                q_edge[hd, rev] = jnp.exp(lg * to_entry)
                k_edge[hd, rev] = jnp.exp(lg * to_exit)
                chunk_decay[hd, rev] = jnp.exp(lg * float(c))

    def step(n, accumulate):
        units = []
        for hd in range(heads):
            for u in range(group):
                for rev in (False, True):
                    chunk = n * group + u
                    if rev:
                        chunk = n_chunks - 1 - chunk
                    units.append((hd, rev, pl.ds(pl.multiple_of(chunk * c, c), c)))
        vs = [v_ref[hd, rows, :].astype(bf) for hd, _, rows in units]
        if scalar_decay:
            qs = [q_ref[hd, rows, :] for hd, _, rows in units]
            ks = [k_refs[rev][hd, rows, :] for hd, rev, rows in units]
            ops = [dict(q_state=(q * q_edge[hd, rev]).astype(bf), k_state=(k * k_edge[hd, rev]).astype(bf),
                        decay=chunk_decay[hd, rev]) for q, k, (hd, rev, _) in zip(qs, ks, units)]
            qk = [lax.dot_general(q.astype(bf), k.astype(bf), _NT, preferred_element_type=jnp.float32)
                  for q, k in zip(qs, ks)]
            scores = [(s * pair_decay[hd, rev]).astype(bf) for s, (hd, rev, _) in zip(qk, units)]
        else:
            lfs = [(lfb_ref if rev else lff_ref)[hd, rows, :] for hd, rev, rows in units]
            css = [_chunk_running_sum(lf, rev) for lf, (_, rev, _) in zip(lfs, units)]
            ops = [_chunk_operands(q_ref[hd, rows, :], k_refs[rev][hd, rows, :], cs, rev)
                   for cs, (hd, rev, rows) in zip(css, units)]
            diag = [lax.dot_general(p["qd"], p["kd"], _NT, preferred_element_type=jnp.float32) for p in ops]
            cross = [lax.dot_general(p["q_cat"], p["k_cat"], _NT, preferred_element_type=jnp.float32) for p in ops]
            scores = [(jnp.where(dmask[rev], d, 0.0) + x).astype(bf)
                      for d, x, (_, rev, _) in zip(diag, cross, units)]
        kv = [lax.dot_general(v, p["k_state"], _TN, preferred_element_type=jnp.float32) for v, p in zip(vs, ops)]
        intra = [jnp.dot(s, v, preferred_element_type=jnp.float32) for s, v in zip(scores, vs)]
        for hd in range(heads):
            for rev in (False, True):
                st = st_refs[rev][hd]
                for i, (h_i, r, rows) in enumerate(units):
                    if (h_i, r) != (hd, rev):
                        continue
                    o = intra[i] + lax.dot_general(ops[i]["q_state"], st.astype(bf), _NT,
                                                   preferred_element_type=jnp.float32)
                    st = st * ops[i]["decay"] + kv[i]
                    if accumulate:
                        o_ref[hd, rows, :] += o
                    else:
                        o_ref[hd, rows, :] = o
                st_refs[rev][hd] = st

    steps = n_chunks // group

    def first(n, carry):
        step(n, False)
        return carry

    def second(n, carry):
        step(n, True)
        return carry

    lax.fori_loop(0, steps // 2, first, 0)
    lax.fori_loop(steps // 2, steps, second, 0)

    tr = min(SCAN_EPILOGUE_ROWS, n_chunks * c)

    def epilogue(i, carry):
        rows = pl.ds(pl.multiple_of(i * tr, tr), tr)
        for hd in range(heads):
            o = o_ref[hd, rows, :]
            g = gate_ref[hd, rows, :]
            y = o * lax.rsqrt(jnp.mean(o * o, axis=-1, keepdims=True) + RMS_EPS) * nw_ref[...]
            y_ref[rows, hd * dv:(hd + 1) * dv] = (y * (g * jax.nn.sigmoid(g))).astype(y_ref.dtype)
        return carry

    lax.fori_loop(0, n_chunks * c // tr, epilogue, 0)


def _bidir_scan(q, k_f, k_b, v, lf_f, lf_b, gate, norm_w, s0_f, s0_b, n_heads, log_decay=None):
    b, _, l, dk = q.shape
    dv = v.shape[-1]
    n_chunks = l // CHUNK
    group = min(SCAN_GROUP, n_chunks // 2)
    heads = min(n_heads, SCAN_GROUP // group)
    assert l % CHUNK == 0 and n_chunks % (2 * group) == 0 and n_heads % heads == 0
    scalar_decay = log_decay is not None
    seq_k = pl.BlockSpec((None, heads, l, dk), lambda bi, hi: (bi, hi, 0, 0))
    seq_v = pl.BlockSpec((None, heads, l, dv), lambda bi, hi: (bi, hi, 0, 0))
    out_y = pl.BlockSpec((None, l, heads * dv), lambda bi, hi: (bi, 0, hi))
    state = pl.BlockSpec((None, heads, dv, dk), lambda bi, hi: (bi, hi, 0, 0))
    if scalar_decay:
        lg = jnp.broadcast_to(log_decay.astype(jnp.float32)[:, :, None, None], (2, n_heads, 8, dk))
        decay_args = (lg,)
        decay_specs = [pl.BlockSpec((2, heads, 8, dk), lambda bi, hi: (0, hi, 0, 0))]
    else:
        decay_args = (lf_f, lf_b)
        decay_specs = [seq_k, seq_k]
    return pl.pallas_call(
        functools.partial(_bidir_scan_body, n_chunks=n_chunks, group=group, heads=heads,
                          scalar_decay=scalar_decay),
        grid=(b, n_heads // heads),
        in_specs=[seq_k, seq_k, seq_k, seq_v] + decay_specs + [
            seq_v, pl.BlockSpec((1, dv), lambda bi, hi: (0, 0)), state, state],
        out_specs=[out_y, state, state],
        out_shape=[jax.ShapeDtypeStruct((b, l, n_heads * dv), jnp.bfloat16),
                   jax.ShapeDtypeStruct((b, n_heads, dv, dk), jnp.float32),
                   jax.ShapeDtypeStruct((b, n_heads, dv, dk), jnp.float32)],
        scratch_shapes=[pltpu.VMEM((heads, l, dv), jnp.float32)],
        compiler_params=pltpu.CompilerParams(
            dimension_semantics=("parallel", "parallel"),
            vmem_limit_bytes=VMEM_LIMIT_BYTES),
        name="bidir_scan",
    )(q, k_f, k_b, v, *decay_args, gate, norm_w.astype(jnp.float32).reshape(1, dv), s0_f, s0_b)


def _log_sigmoid(x):
    return jnp.minimum(x, 0.0) - jnp.log(1.0 + jnp.exp(-jnp.abs(x)))


def _store_heads(ref, t):
    n_heads, _, dh = ref.shape
    for hh in range(n_heads):
        ref[hh] = t[:, hh * dh:(hh + 1) * dh].astype(ref.dtype)


def _modulated_bf16(x_ref, sc_ref, sh_ref):
    return (x_ref[...] * (1.0 + sc_ref[...]) + sh_ref[...]).astype(jnp.bfloat16)


def _rope_heads(t, cos2, sin2, n_heads):
    dk = t.shape[-1] // n_heads
    out = []
    for hh in range(n_heads):
        th = t[:, hh * dk:(hh + 1) * dk]
        out.append(th * cos2 + pltpu.roll(th, dk // 2, axis=1) * sin2)
    return jnp.concatenate(out, axis=-1)


def _even_in_body(*refs, use_rope):
    if use_rope:
        x_ref, sc_ref, sh_ref, w_ref, lbf_ref, omlb_ref, cos_ref, sin_ref = refs[:8]
    else:
        x_ref, sc_ref, sh_ref, w_ref, lbf_ref, omlb_ref = refs[:6]
    (qa_ref, kaf_ref, kab_ref, lff_ref, lfb_ref, ia_ref, ga_ref, qb_ref, kb_ref, vb_ref, gb_ref) = refs[-11:]
    h = _modulated_bf16(x_ref, sc_ref, sh_ref)
    offs = np.cumsum((0,) + EVEN_SIZES).tolist()

    def seg(i):
        return jnp.dot(h, w_ref[:, offs[i]:offs[i + 1]], preferred_element_type=jnp.float32)

    _store_heads(qa_ref, seg(0) * HA_DK ** -0.5)
    for d, (lf_ref, key_ref) in enumerate(((lff_ref, kaf_ref), (lfb_ref, kab_ref))):
        z = seg(1 + d)
        e = jnp.exp(-jnp.abs(z))
        big = 1.0 / (1.0 + e)
        small = e * big
        pos = z >= 0.0
        sig_neg = jnp.where(pos, small, big)
        _store_heads(lf_ref, jnp.log(jnp.where(pos, big, small) + lbf_ref[d:d + 1, :] * sig_neg))
        _store_heads(key_ref, omlb_ref[d:d + 1, :] * sig_neg)
    _store_heads(ia_ref, seg(3))
    _store_heads(ga_ref, seg(4))
    qb = seg(5) * RB_DK ** -0.5
    kb = seg(6)
    if use_rope:
        qb = _rope_heads(qb, cos_ref[...], sin_ref[...], RB_HEADS)
        kb = _rope_heads(kb, cos_ref[...], sin_ref[...], RB_HEADS)
    _store_heads(qb_ref, qb)
    _store_heads(kb_ref, kb)
    _store_heads(vb_ref, seg(7))
    _store_heads(gb_ref, seg(8))


def _row_tiled_call(body, name, x, sc, sh, consts, seq_inputs, n_heads, out_widths, out_dtypes):
    b, l, d = x.shape
    tm = min(l, PROJ_ROWS)
    assert l % tm == 0

    def mod_spec(m):
        if m.shape[0] == b:
            return pl.BlockSpec((None, 1, d), lambda bi, ri: (bi, 0, 0))
        return pl.BlockSpec((None, 1, d), lambda bi, ri: (0, 0, 0))

    def const_spec(a):
        return pl.BlockSpec(a.shape, lambda bi, ri, nd=a.ndim: (0,) * nd)

    return pl.pallas_call(
        body,
        grid=(b, l // tm),
        in_specs=[pl.BlockSpec((None, tm, d), lambda bi, ri: (bi, ri, 0)), mod_spec(sc), mod_spec(sh)]
        + [const_spec(a) for a in consts]
        + [pl.BlockSpec((tm, a.shape[-1]), lambda bi, ri: (ri, 0)) for a in seq_inputs],
        out_specs=[pl.BlockSpec((None, n_heads, tm, w // n_heads), lambda bi, ri: (bi, 0, ri, 0))
                   for w in out_widths],
        out_shape=[jax.ShapeDtypeStruct((b, n_heads, l, w // n_heads), dt)
                   for w, dt in zip(out_widths, out_dtypes)],
        compiler_params=pltpu.CompilerParams(
            dimension_semantics=("parallel", "parallel"),
            vmem_limit_bytes=VMEM_LIMIT_BYTES),
        name=name,
    )(x, sc, sh, *consts, *seq_inputs)


def _even_mixer(x, sc, sh, w_in, lb, ha_norm_w, rb_log_gamma, rope, states):
    f32, bf = jnp.float32, jnp.bfloat16
    assert HA_HEADS == RB_HEADS
    consts = [w_in.astype(bf), jnp.maximum(lb, LB_FLOOR), 1.0 - lb]
    seq = []
    if rope is not None:
        cos, sin = rope
        seq = [jnp.concatenate([cos, cos], axis=-1), jnp.concatenate([-sin, sin], axis=-1)]
    qa, ka_f, ka_b, lf_f, lf_b, ia, ga, qb, kb, vb, gb = _row_tiled_call(
        functools.partial(_even_in_body, use_rope=rope is not None), "even_in_proj", x, sc, sh, consts, seq,
        HA_HEADS, (HA_KEY,) * 5 + (HA_VAL, HA_VAL, RB_KEY, RB_KEY, RB_VAL, RB_VAL),
        (f32,) * 5 + (bf, f32, f32, f32, bf, f32))
    ya, sa_f, sa_b = _bidir_scan(qa, ka_f, ka_b, ia, lf_f, lf_b, ga, ha_norm_w, states[0][0], states[0][1],
                                 HA_HEADS)
    yb, sb_f, sb_b = _bidir_scan(qb, kb, kb, vb, None, None, gb, jnp.ones((RB_DV,), f32),
                                 states[1][0], states[1][1], RB_HEADS, log_decay=rb_log_gamma)
    return (ya, yb), ((sa_f, sa_b), (sb_f, sb_b))


def _odd_in_body(x_ref, sc_ref, sh_ref, w_ref, w2_ref, b2_ref, q_ref, k_ref, v_ref, g_ref, lff_ref, lfb_ref):
    h = _modulated_bf16(x_ref, sc_ref, sh_ref)
    offs = np.cumsum((0,) + ODD_SIZES[:4]).tolist()

    def seg(lo, hi):
        return jnp.dot(h, w_ref[:, lo:hi], preferred_element_type=jnp.float32)

    _store_heads(q_ref, seg(offs[0], offs[1]) * GC_DK ** -0.5)
    _store_heads(k_ref, seg(offs[1], offs[2]))
    _store_heads(v_ref, seg(offs[2], offs[3]))
    _store_heads(g_ref, seg(offs[3], offs[4]))
    lr = seg(offs[4], w_ref.shape[-1]).astype(jnp.bfloat16)
    for d, lf_ref in enumerate((lff_ref, lfb_ref)):
        zz = jnp.dot(lr, w2_ref[d], preferred_element_type=jnp.float32) + b2_ref[d:d + 1, :]
        _store_heads(lf_ref, _log_sigmoid(zz) / GC_TAU)


def _odd_mixer(x, sc, sh, w_in, w2, b2, norm_w, states):
    f32, bf = jnp.float32, jnp.bfloat16
    lane = 128
    pad = lane - 2 * GC_RANK
    w_pad = jnp.pad(w_in, ((0, 0), (0, pad))).astype(bf)
    w2_pad = jnp.stack([jnp.pad(w2[0], ((0, lane - GC_RANK), (0, 0))),
                        jnp.pad(w2[1], ((GC_RANK, lane - 2 * GC_RANK), (0, 0)))]).astype(bf)
    q, k, v, g, lf_f, lf_b = _row_tiled_call(
        _odd_in_body, "odd_in_proj", x, sc, sh, [w_pad, w2_pad, b2], [], GC_HEADS,
        (GC_KEY, GC_KEY, GC_VAL, GC_VAL, GC_KEY, GC_KEY), (f32, f32, bf, f32, f32, f32))
    y, s_f, s_b = _bidir_scan(q, k, k, v, lf_f, lf_b, g, norm_w, states[0], states[1], GC_HEADS)
    return (y,), (s_f, s_b)


def _mixer_out_body(*refs, n_y):
    y_refs = refs[:n_y]
    (wo_ref, x_ref, g1_ref, lnw_ref, lnb_ref, sc2_ref, sh2_ref, wr_ref, x1_ref, h2_ref, aff_ref) = refs[n_y:]
    mix = None
    lo = 0
    for y_ref in y_refs:
        hi = lo + y_ref.shape[-1]
        part = jnp.dot(y_ref[...], wo_ref[lo:hi, :], preferred_element_type=jnp.float32)
        mix = part if mix is None else mix + part
        lo = hi
    z = DN_ALPHA * x_ref[...] + g1_ref[...] * mix
    mu = jnp.mean(z, axis=-1, keepdims=True)
    zc = z - mu
    var = jnp.mean(zc * zc, axis=-1, keepdims=True)
    x1 = zc * lax.rsqrt(var + LN_EPS) * lnw_ref[...] + lnb_ref[...]
    x1_ref[...] = x1
    h2 = (x1 * (1.0 + sc2_ref[...]) + sh2_ref[...]).astype(jnp.bfloat16)
    h2_ref[...] = h2
    logits = lax.dot_general(wr_ref[...], h2, _NT, preferred_element_type=jnp.float32)
    ex = jnp.exp(logits - jnp.max(logits, axis=0, keepdims=True))
    aff_ref[...] = ex / jnp.sum(ex, axis=0, keepdims=True)


def _mixer_out(ys, w_out, x, g1, ln_w, ln_b, sc2, sh2, w_router):
    b, l, d = x.shape
    e = w_router.shape[-1]
    tm = min(l, MIXER_OUT_ROWS)
    assert l % tm == 0

    def mod_spec(m):
        if m.shape[0] == b:
            return pl.BlockSpec((None, 1, d), lambda bi, ri: (bi, 0, 0))
        return pl.BlockSpec((None, 1, d), lambda bi, ri: (0, 0, 0))

    def const_spec(a):
        return pl.BlockSpec(a.shape, lambda bi, ri, nd=a.ndim: (0,) * nd)

    def rows(w):
        return pl.BlockSpec((None, tm, w), lambda bi, ri: (bi, ri, 0))

    consts_a = [w_out.astype(jnp.bfloat16)]
    consts_b = [ln_w.reshape(1, d), ln_b.reshape(1, d)]
    wr_t = w_router.T.astype(jnp.bfloat16)
    return pl.pallas_call(
        functools.partial(_mixer_out_body, n_y=len(ys)),
        grid=(b, l // tm),
        in_specs=[rows(y.shape[-1]) for y in ys] + [const_spec(consts_a[0]), rows(d), mod_spec(g1)]
        + [const_spec(a) for a in consts_b] + [mod_spec(sc2), mod_spec(sh2), const_spec(wr_t)],
        out_specs=[rows(d), rows(d), pl.BlockSpec((None, e, tm), lambda bi, ri: (bi, 0, ri))],
        out_shape=[jax.ShapeDtypeStruct((b, l, d), jnp.float32), jax.ShapeDtypeStruct((b, l, d), jnp.bfloat16),
                   jax.ShapeDtypeStruct((b, e, l), jnp.float32)],
        compiler_params=pltpu.CompilerParams(
            dimension_semantics=("parallel", "parallel"),
            vmem_limit_bytes=VMEM_LIMIT_BYTES),
        name="mixer_out",
    )(*ys, *consts_a, x, g1, *consts_b, sc2, sh2, wr_t)


def _expert_ffn_body(x_ref, gate_ref, wg_ref, wu_ref, wd_ref, out_ref, hid_ref):
    tb, cap, d = x_ref.shape
    ff = wg_ref.shape[-1]
    x = x_ref[...].reshape(tb * cap, d)
    for j in range(ff // FFN_FF_TILE):
        cols = slice(j * FFN_FF_TILE, (j + 1) * FFN_FF_TILE)
        g = jnp.dot(x, wg_ref[:, cols], preferred_element_type=jnp.float32)
        u = jnp.dot(x, wu_ref[:, cols], preferred_element_type=jnp.float32)
        hid_ref[:, cols] = (g * jax.nn.sigmoid(g) * u).astype(jnp.bfloat16)
    y = jnp.dot(hid_ref[...], wd_ref[...], preferred_element_type=jnp.float32)
    out_ref[...] = (y.reshape(tb, cap, d) * gate_ref[...]).astype(out_ref.dtype)


def _expert_ffn(xs, gate, layer, w_gate, w_up, w_down):
    b, e, cap, d = xs.shape
    ff = w_gate.shape[-1]
    tb = max(1, min(b, FFN_ROWS_PER_STEP // cap))
    assert b % tb == 0 and ff % FFN_FF_TILE == 0
    return pl.pallas_call(
        _expert_ffn_body,
        grid=(e, b // tb),
        in_specs=[
            pl.BlockSpec((tb, None, cap, d), lambda ei, bi: (bi, ei, 0, 0)),
            pl.BlockSpec((tb, None, cap, 1), lambda ei, bi: (bi, ei, 0, 0)),
            pl.BlockSpec((None, None, d, ff), lambda ei, bi: (layer, ei, 0, 0)),
            pl.BlockSpec((None, None, d, ff), lambda ei, bi: (layer, ei, 0, 0)),
            pl.BlockSpec((None, None, ff, d), lambda ei, bi: (layer, ei, 0, 0)),
        ],
        out_specs=pl.BlockSpec((tb, None, cap, d), lambda ei, bi: (bi, ei, 0, 0)),
        out_shape=jax.ShapeDtypeStruct((b, e, cap, d), jnp.bfloat16),
        scratch_shapes=[pltpu.VMEM((tb * cap, ff), jnp.bfloat16)],
        compiler_params=pltpu.CompilerParams(
            dimension_semantics=("parallel", "parallel"),
            vmem_limit_bytes=VMEM_LIMIT_BYTES),
        name="expert_ffn",
    )(xs, gate, w_gate, w_up, w_down)


def _layer_norm_rows(z, w, b):
    mu = jnp.mean(z, axis=-1, keepdims=True)
    zc = z - mu
    var = jnp.mean(zc * zc, axis=-1, keepdims=True)
    return zc * lax.rsqrt(var + LN_EPS) * w + b


def _combine_ln_body(lohi_ref, idx_ref, idx_all_ref, ys_ref, x_ref, g_ref, w_ref, b_ref, out_ref, ffn_ref, *,
                     window):
    tr, d = x_ref.shape
    e, cap, _ = ys_ref.shape
    bi, ri = pl.program_id(0), pl.program_id(1)
    n_bounds = pl.num_programs(1) + 1
    tokens = lax.broadcasted_iota(jnp.int32, (tr, 1), 0) + ri * tr

    def all_slots():
        onehot = jnp.where(tokens == idx_all_ref[...], 1.0, 0.0).astype(jnp.bfloat16)
        return jnp.dot(onehot, ys_ref[...].reshape(e * cap, d), preferred_element_type=jnp.float32)

    def finish(ffn):
        z = DN_ALPHA * x_ref[...] + g_ref[...] * ffn
        out_ref[...] = _layer_norm_rows(z, w_ref[...], b_ref[...])

    if window == cap:
        finish(all_slots())
        return
    per_dot = CONTRACT_DEPTH // window
    ffn = jnp.zeros((tr, d), jnp.float32)
    overflow = jnp.int32(0)
    for e0 in range(0, e, per_dot):
        wins, rows = [], []
        for ei in range(e0, e0 + per_dot):
            lo = lohi_ref[bi, ei * n_bounds + ri]
            hi = lohi_ref[bi, ei * n_bounds + ri + 1]
            start = jnp.minimum((lo // BF16_SUBLANES) * BF16_SUBLANES, cap - window)
            overflow = overflow + (hi > start + window).astype(jnp.int32)
            wins.append(pltpu.roll(idx_ref[ei], (cap - start) % cap, axis=1)[:, :window])
            rows.append(ys_ref[ei, pl.ds(pl.multiple_of(start, BF16_SUBLANES), window), :])
        onehot = jnp.where(tokens == jnp.concatenate(wins, axis=1), 1.0, 0.0).astype(jnp.bfloat16)
        ffn = ffn + jnp.dot(onehot, jnp.concatenate(rows, axis=0), preferred_element_type=jnp.float32)
    ffn_ref[...] = ffn

    @pl.when(overflow > 0)
    def _():
        ffn_ref[...] = all_slots()

    finish(ffn_ref[...])


def _combine_ln(ys, idx, x, g, w, b):
    bsz, e, cap, d = ys.shape
    n = x.shape[1]
    tr = min(n, COMBINE_ROWS)
    window = min(cap, COMBINE_WINDOW)
    assert n % tr == 0 and CONTRACT_DEPTH % window == 0 and e % (CONTRACT_DEPTH // window) == 0
    assert cap % BF16_SUBLANES == 0
    per_sample = g.shape[0] == bsz
    bounds = jnp.arange(0, n + 1, tr, dtype=jnp.int32)
    lohi = jnp.sum(idx[..., None] < bounds, axis=2, dtype=jnp.int32).reshape(bsz, e * (n // tr + 1))
    grid_spec = pltpu.PrefetchScalarGridSpec(
        num_scalar_prefetch=1,
        grid=(bsz, n // tr),
        in_specs=[
            pl.BlockSpec((None, e, 1, cap), lambda bi, ri, lohi: (bi, 0, 0, 0)),
            pl.BlockSpec((None, 1, e * cap), lambda bi, ri, lohi: (bi, 0, 0)),
            pl.BlockSpec((None, e, cap, d), lambda bi, ri, lohi: (bi, 0, 0, 0)),
            pl.BlockSpec((None, tr, d), lambda bi, ri, lohi: (bi, ri, 0)),
            pl.BlockSpec((None, 1, d), (lambda bi, ri, lohi: (bi, 0, 0)) if per_sample
                         else (lambda bi, ri, lohi: (0, 0, 0))),
            pl.BlockSpec((1, d), lambda bi, ri, lohi: (0, 0)),
            pl.BlockSpec((1, d), lambda bi, ri, lohi: (0, 0)),
        ],
        out_specs=pl.BlockSpec((None, tr, d), lambda bi, ri, lohi: (bi, ri, 0)),
        scratch_shapes=[pltpu.VMEM((tr, d), jnp.float32)],
    )
    return pl.pallas_call(
        functools.partial(_combine_ln_body, window=window),
        grid_spec=grid_spec,
        out_shape=jax.ShapeDtypeStruct((bsz, n, d), jnp.float32),
        compiler_params=pltpu.CompilerParams(
            dimension_semantics=("parallel", "parallel"),
            vmem_limit_bytes=VMEM_LIMIT_BYTES),
        name="combine_ln",
    )(lohi, idx.reshape(bsz, e, 1, cap), idx.reshape(bsz, 1, e * cap), ys, x, g, w.reshape(1, d), b.reshape(1, d))


def _ec_ffn_ln(x, h, aff, g, layer, w_gate, w_up, w_down, ln_w, ln_b):
    n = x.shape[1]
    cap = EC_CAPACITY_FACTOR * n // N_EXPERTS
    gate, idx = lax.top_k(aff, cap)
    idx, gate = lax.sort((idx, gate), dimension=2, num_keys=1)
    xs = jax.vmap(lambda hb, ib: hb[ib])(h, idx)
    ys = _expert_ffn(xs, gate[..., None], layer, w_gate, w_up, w_down)
    return _combine_ln(ys, idx, x, g, ln_w, ln_b)


def kernel(x, c, ctx, c_ctx, ada_w, ada_b, ln_w, ln_b, even_w_in, even_w_out, ha_lb, ha_norm,
           rb_decay, odd_w_in, odd_w_out, gc_w2, gc_b2, gc_norm, router_w, exp_w_gate, exp_w_up,
           exp_w_down):
    n_lat = x.shape[1]
    rows = n_lat // GRID_W
    rope = _axial_rope(rows)
    b_ctx = ctx.shape[0]
    lb_p = jax.nn.softmax(ha_lb, axis=0)
    lb_all = jnp.cumsum(lb_p, axis=0) - lb_p[0]
    cond_lat = jax.nn.silu(c)
    cond_ctx = jax.nn.silu(c_ctx)
    w_gate_bf, w_up_bf, w_down_bf = (w.astype(jnp.bfloat16) for w in (exp_w_gate, exp_w_up, exp_w_down))

    for l in range(DEPTH):
        last = l == DEPTH - 1
        mod_lat = (cond_lat @ ada_w[l] + ada_b[l])[:, None, :]
        mod_ctx = (cond_ctx @ ada_w[l] + ada_b[l])[None, None, :]
        sh1, sc1, g1, sh2, sc2, g2 = jnp.split(mod_lat, 6, axis=-1)
        csh1, csc1, cg1, csh2, csc2, cg2 = jnp.split(mod_ctx, 6, axis=-1)

        j = l // 2
        if l % 2 == 0:
            z_a = jnp.zeros((b_ctx, HA_HEADS, HA_DV, HA_DK), jnp.float32)
            z_b = jnp.zeros((b_ctx, RB_HEADS, RB_DV, RB_DK), jnp.float32)
            log_gamma = jax.nn.log_sigmoid(rb_decay[j])
            w_out = even_w_out[j]
            y_ctx, st = _even_mixer(ctx, csc1, csh1, even_w_in[j], lb_all[j], ha_norm[j], log_gamma, None,
                                    ((z_a, z_a), (z_b, z_b)))
            y_lat, _ = _even_mixer(x, sc1, sh1, even_w_in[j], lb_all[j], ha_norm[j], log_gamma, rope, st)
        else:
            z_c = jnp.zeros((b_ctx, GC_HEADS, GC_DV, GC_DK), jnp.float32)
            w_out = odd_w_out[j]
            y_ctx, st = _odd_mixer(ctx, csc1, csh1, odd_w_in[j], gc_w2[j], gc_b2[j], gc_norm[j], (z_c, z_c))
            y_lat, _ = _odd_mixer(x, sc1, sh1, odd_w_in[j], gc_w2[j], gc_b2[j], gc_norm[j], st)

        experts = (l, w_gate_bf, w_up_bf, w_down_bf)
        x, h2, aff = _mixer_out(y_lat, w_out, x, g1, ln_w[l, 0], ln_b[l, 0], sc2, sh2, router_w[l])
        x = _ec_ffn_ln(x, h2, aff, g2, *experts, ln_w[l, 1], ln_b[l, 1])
        if not last:
            ctx, h2, aff = _mixer_out(y_ctx, w_out, ctx, cg1, ln_w[l, 0], ln_b[l, 0], csc2, csh2, router_w[l])
            ctx = _ec_ffn_ln(ctx, h2, aff, cg2, *experts, ln_w[l, 1], ln_b[l, 1])
    return x
```

```python
import functools

import jax
import jax.numpy as jnp
import numpy as np
from jax import lax
from jax.experimental import pallas as pl
from jax.experimental.pallas import tpu as pltpu

D_MODEL = 1024
DEPTH = 4
GRID_W = 64

HA_HEADS = 4
HA_DK = 128
HA_DV = 128
RB_HEADS = 4
RB_DK = 128
RB_DV = 128
GC_HEADS = 4
GC_DK = 128
GC_DV = 256
GC_RANK = 16
GC_TAU = 16.0
N_EXPERTS = 16
EXPERT_FF = 2816
EC_CAPACITY_FACTOR = 2

CHUNK = 64
SUB = 16
N_SUB = CHUNK // SUB
ROPE_BASE = 10000.0
LN_EPS = 1e-5
RMS_EPS = 1e-6
LB_FLOOR = 1e-30
DN_ALPHA = (2 * DEPTH) ** 0.25
LOG2_E = 1.4426950408889634

HA_KEY = HA_HEADS * HA_DK
HA_VAL = HA_HEADS * HA_DV
RB_KEY = RB_HEADS * RB_DK
RB_VAL = RB_HEADS * RB_DV
GC_KEY = GC_HEADS * GC_DK
GC_VAL = GC_HEADS * GC_DV
EVEN_SIZES = (HA_KEY, HA_KEY, HA_KEY, HA_VAL, HA_VAL, RB_KEY, RB_KEY, RB_VAL, RB_VAL)
ODD_SIZES = (GC_KEY, GC_KEY, GC_VAL, GC_VAL, 2 * GC_RANK)

VMEM_LIMIT_BYTES = 56 * 1024 * 1024
FFN_ROWS_PER_STEP = 512
FFN_FF_TILE = 256
SCAN_EPILOGUE_ROWS = 256
COMBINE_ROWS = 256
COMBINE_WINDOW = 64
CONTRACT_DEPTH = 256
BF16_SUBLANES = 16
PROJ_ROWS = 512
MIXER_OUT_ROWS = 1024
SCAN_GROUP = 16

_NT = (((1,), (1,)), ((), ()))
_TN = (((0,), (0,)), ((), ()))


def _axial_rope(rows):
    r_idx, c_idx = jnp.meshgrid(jnp.arange(rows), jnp.arange(GRID_W), indexing='ij')
    n_freq = RB_DK // 4
    freq = ROPE_BASE ** (-jnp.arange(n_freq, dtype=jnp.float32) / n_freq)
    ang = jnp.concatenate([r_idx.reshape(-1, 1).astype(jnp.float32) * freq,
                           c_idx.reshape(-1, 1).astype(jnp.float32) * freq], axis=-1)
    return jnp.cos(ang), jnp.sin(ang)


def _chunk_running_sum(lf, rev):
    c, dk = lf.shape
    row = lax.broadcasted_iota(jnp.int32, (c, dk), 0)
    x = lf
    shift = 1
    while shift < c:
        if rev:
            x = x + jnp.where(row < c - shift, pltpu.roll(x, c - shift, axis=0), 0.0)
        else:
            x = x + jnp.where(row >= shift, pltpu.roll(x, shift, axis=0), 0.0)
        shift *= 2
    return x


def _chunk_operands(q, k, cs, rev):
    bf = jnp.bfloat16
    dk = q.shape[-1]
    cs = cs * LOG2_E
    zero_row = jnp.zeros((1, dk), jnp.float32)
    if not rev:
        ends = [cs[SUB * a + SUB - 1:SUB * a + SUB, :] for a in range(N_SUB)]
        starts = [zero_row] + ends[:-1]
        mids = [cs[SUB * a + SUB // 2 - 1:SUB * a + SUB // 2, :] for a in range(N_SUB)]
        order = list(range(N_SUB))
    else:
        ends = [cs[SUB * a:SUB * a + 1, :] for a in range(N_SUB)]
        starts = ends[1:] + [zero_row]
        mids = [cs[SUB * a + SUB // 2:SUB * a + SUB // 2 + 1, :] for a in range(N_SUB)]
        order = list(range(N_SUB - 1, -1, -1))
    last = ends[order[-1]]

    def per_block(rows):
        return jnp.concatenate([jnp.broadcast_to(r, (SUB, dk)) for r in rows], axis=0)

    mid_b, start_b, end_b = per_block(mids), per_block(starts), per_block(ends)
    qd = q * jnp.exp2(cs - mid_b)
    kd = k * jnp.exp2(mid_b - cs)
    q_in = q * jnp.exp2(cs - start_b)
    k_out = k * jnp.exp2(end_b - cs)
    q_state = q_in * jnp.exp2(start_b)
    k_state = k_out * jnp.exp2(last - end_b)

    def blk(t, a):
        return t[SUB * a:SUB * (a + 1), :]

    zeros_blk = jnp.zeros((SUB, dk), jnp.float32)
    q_cat, k_cat = [], []
    for s in range(N_SUB - 1):
        src = order[s]
        q_rows = [zeros_blk] * N_SUB
        k_rows = [zeros_blk] * N_SUB
        k_rows[src] = blk(k_out, src)
        for t in range(s + 1, N_SUB):
            tgt = order[t]
            piece = blk(q_in, tgt)
            if t > s + 1:
                piece = piece * jnp.exp2(jnp.minimum(starts[tgt] - ends[src], 0.0))
            q_rows[tgt] = piece
        q_cat.append(jnp.concatenate(q_rows, axis=0))
        k_cat.append(jnp.concatenate(k_rows, axis=0))
    return dict(qd=qd.astype(bf), kd=kd.astype(bf),
                q_cat=jnp.concatenate(q_cat, axis=1).astype(bf), k_cat=jnp.concatenate(k_cat, axis=1).astype(bf),
                q_state=q_state.astype(bf), k_state=k_state.astype(bf), decay=jnp.exp2(last))


def _bidir_scan_body(*refs, n_chunks, group, heads, scalar_decay):
    if scalar_decay:
        (q_ref, kf_ref, kb_ref, v_ref, lg_ref, gate_ref, nw_ref, s0f_ref, s0b_ref,
         y_ref, sf_ref, sb_ref, o_ref) = refs
    else:
        (q_ref, kf_ref, kb_ref, v_ref, lff_ref, lfb_ref, gate_ref, nw_ref, s0f_ref, s0b_ref,
         y_ref, sf_ref, sb_ref, o_ref) = refs
    c = CHUNK
    bf = jnp.bfloat16
    dk = q_ref.shape[-1]
    dv = o_ref.shape[-1]
    row = lax.broadcasted_iota(jnp.int32, (c, c), 0)
    col = lax.broadcasted_iota(jnp.int32, (c, c), 1)
    same_blk = (row // SUB) == (col // SUB)
    tri = {False: (col <= row), True: (col >= row)}
    dmask = {r: same_blk & t for r, t in tri.items()}
    k_refs = {False: kf_ref, True: kb_ref}
    st_refs = {False: sf_ref, True: sb_ref}
    sf_ref[...] = s0f_ref[...]
    sb_ref[...] = s0b_ref[...]
    if scalar_decay:
        pos = lax.broadcasted_iota(jnp.int32, (c, dk), 0).astype(jnp.float32)
        dist = (row - col).astype(jnp.float32)
        pair_decay, q_edge, k_edge, chunk_decay = {}, {}, {}, {}
        for hd in range(heads):
            for rev in (False, True):
                lg = lg_ref[int(rev), hd, 0:1, :]
                to_entry = (c - pos) if rev else (pos + 1.0)
                to_exit = pos if rev else (c - 1.0 - pos)
                signed = -dist if rev else dist
                pair_decay[hd, rev] = jnp.where(tri[rev], jnp.exp(lg[:, :c] * jnp.maximum(signed, 0.0)), 0.0)
                q_edge[hd, rev] = jnp.exp(lg * to_entry)
                k_edge[hd, rev] = jnp.exp(lg * to_exit)
                chunk_decay[hd, rev] = jnp.exp(lg * float(c))

    def step(n, accumulate):
        units = []
        for hd in range(heads):
            for u in range(group):
                for rev in (False, True):
                    chunk = n * group + u
                    if rev:
                        chunk = n_chunks - 1 - chunk
                    units.append((hd, rev, pl.ds(pl.multiple_of(chunk * c, c), c)))
        vs = [v_ref[hd, rows, :].astype(bf) for hd, _, rows in units]
        if scalar_decay:
            qs = [q_ref[hd, rows, :] for hd, _, rows in units]
            ks = [k_refs[rev][hd, rows, :] for hd, rev, rows in units]
            ops = [dict(q_state=(q * q_edge[hd, rev]).astype(bf), k_state=(k * k_edge[hd, rev]).astype(bf),
                        decay=chunk_decay[hd, rev]) for q, k, (hd, rev, _) in zip(qs, ks, units)]
            qk = [lax.dot_general(q.astype(bf), k.astype(bf), _NT, preferred_element_type=jnp.float32)
                  for q, k in zip(qs, ks)]
            scores = [(s * pair_decay[hd, rev]).astype(bf) for s, (hd, rev, _) in zip(qk, units)]
        else:
            lfs = [(lfb_ref if rev else lff_ref)[hd, rows, :] for hd, rev, rows in units]
            css = [_chunk_running_sum(lf, rev) for lf, (_, rev, _) in zip(lfs, units)]
            ops = [_chunk_operands(q_ref[hd, rows, :], k_refs[rev][hd, rows, :], cs, rev)
                   for cs, (hd, rev, rows) in zip(css, units)]
            diag = [lax.dot_general(p["qd"], p["kd"], _NT, preferred_element_type=jnp.float32) for p in ops]
            cross = [lax.dot_general(p["q_cat"], p["k_cat"], _NT, preferred_element_type=jnp.float32) for p in ops]
            scores = [(jnp.where(dmask[rev], d, 0.0) + x).astype(bf)
                      for d, x, (_, rev, _) in zip(diag, cross, units)]
        kv = [lax.dot_general(v, p["k_state"], _TN, preferred_element_type=jnp.float32) for v, p in zip(vs, ops)]
        intra = [jnp.dot(s, v, preferred_element_type=jnp.float32) for s, v in zip(scores, vs)]
        for hd in range(heads):
            for rev in (False, True):
                st = st_refs[rev][hd]
                for i, (h_i, r, rows) in enumerate(units):
                    if (h_i, r) != (hd, rev):
                        continue
                    o = intra[i] + lax.dot_general(ops[i]["q_state"], st.astype(bf), _NT,
                                                   preferred_element_type=jnp.float32)
                    st = st * ops[i]["decay"] + kv[i]
                    if accumulate:
                        o_ref[hd, rows, :] += o
                    else:
                        o_ref[hd, rows, :] = o
                st_refs[rev][hd] = st

    steps = n_chunks // group

    def first(n, carry):
        step(n, False)
        return carry

    def second(n, carry):
        step(n, True)
        return carry

    lax.fori_loop(0, steps // 2, first, 0)
    lax.fori_loop(steps // 2, steps, second, 0)

    tr = min(SCAN_EPILOGUE_ROWS, n_chunks * c)

    def epilogue(i, carry):
        rows = pl.ds(pl.multiple_of(i * tr, tr), tr)
        for hd in range(heads):
            o = o_ref[hd, rows, :]
            g = gate_ref[hd, rows, :]
            y = o * lax.rsqrt(jnp.mean(o * o, axis=-1, keepdims=True) + RMS_EPS) * nw_ref[...]
            y_ref[rows, hd * dv:(hd + 1) * dv] = (y * (g * jax.nn.sigmoid(g))).astype(y_ref.dtype)
        return carry

    lax.fori_loop(0, n_chunks * c // tr, epilogue, 0)


def _bidir_scan(q, k_f, k_b, v, lf_f, lf_b, gate, norm_w, s0_f, s0_b, n_heads, log_decay=None):
    b, _, l, dk = q.shape
    dv = v.shape[-1]
    n_chunks = l // CHUNK
    group = min(SCAN_GROUP, n_chunks // 2)
    heads = min(n_heads, SCAN_GROUP // group)
    assert l % CHUNK == 0 and n_chunks % (2 * group) == 0 and n_heads % heads == 0
    scalar_decay = log_decay is not None
    seq_k = pl.BlockSpec((None, heads, l, dk), lambda bi, hi: (bi, hi, 0, 0))
    seq_v = pl.BlockSpec((None, heads, l, dv), lambda bi, hi: (bi, hi, 0, 0))
    out_y = pl.BlockSpec((None, l, heads * dv), lambda bi, hi: (bi, 0, hi))
    state = pl.BlockSpec((None, heads, dv, dk), lambda bi, hi: (bi, hi, 0, 0))
    if scalar_decay:
        lg = jnp.broadcast_to(log_decay.astype(jnp.float32)[:, :, None, None], (2, n_heads, 8, dk))
        decay_args = (lg,)
        decay_specs = [pl.BlockSpec((2, heads, 8, dk), lambda bi, hi: (0, hi, 0, 0))]
    else:
        decay_args = (lf_f, lf_b)
        decay_specs = [seq_k, seq_k]
    return pl.pallas_call(
        functools.partial(_bidir_scan_body, n_chunks=n_chunks, group=group, heads=heads,
                          scalar_decay=scalar_decay),
        grid=(b, n_heads // heads),
        in_specs=[seq_k, seq_k, seq_k, seq_v] + decay_specs + [
            seq_v, pl.BlockSpec((1, dv), lambda bi, hi: (0, 0)), state, state],
        out_specs=[out_y, state, state],
        out_shape=[jax.ShapeDtypeStruct((b, l, n_heads * dv), jnp.bfloat16),
                   jax.ShapeDtypeStruct((b, n_heads, dv, dk), jnp.float32),
                   jax.ShapeDtypeStruct((b, n_heads, dv, dk), jnp.float32)],
        scratch_shapes=[pltpu.VMEM((heads, l, dv), jnp.float32)],
        compiler_params=pltpu.CompilerParams(
            dimension_semantics=("parallel", "parallel"),
            vmem_limit_bytes=VMEM_LIMIT_BYTES),
        name="bidir_scan",
    )(q, k_f, k_b, v, *decay_args, gate, norm_w.astype(jnp.float32).reshape(1, dv), s0_f, s0_b)


def _log_sigmoid(x):
    return jnp.minimum(x, 0.0) - jnp.log(1.0 + jnp.exp(-jnp.abs(x)))


def _store_heads(ref, t):
    n_heads, _, dh = ref.shape
    for hh in range(n_heads):
        ref[hh] = t[:, hh * dh:(hh + 1) * dh].astype(ref.dtype)


def _modulated_bf16(x_ref, sc_ref, sh_ref):
    return (x_ref[...] * (1.0 + sc_ref[...]) + sh_ref[...]).astype(jnp.bfloat16)


def _rope_heads(t, cos2, sin2, n_heads):
    dk = t.shape[-1] // n_heads
    out = []
    for hh in range(n_heads):
        th = t[:, hh * dk:(hh + 1) * dk]
        out.append(th * cos2 + pltpu.roll(th, dk // 2, axis=1) * sin2)
    return jnp.concatenate(out, axis=-1)


def _even_in_body(*refs, use_rope):
    if use_rope:
        x_ref, sc_ref, sh_ref, w_ref, lbf_ref, omlb_ref, cos_ref, sin_ref = refs[:8]
    else:
        x_ref, sc_ref, sh_ref, w_ref, lbf_ref, omlb_ref = refs[:6]
    (qa_ref, kaf_ref, kab_ref, lff_ref, lfb_ref, ia_ref, ga_ref, qb_ref, kb_ref, vb_ref, gb_ref) = refs[-11:]
    h = _modulated_bf16(x_ref, sc_ref, sh_ref)
    offs = np.cumsum((0,) + EVEN_SIZES).tolist()

    def seg(i):
        return jnp.dot(h, w_ref[:, offs[i]:offs[i + 1]], preferred_element_type=jnp.float32)

    _store_heads(qa_ref, seg(0) * HA_DK ** -0.5)
    for d, (lf_ref, key_ref) in enumerate(((lff_ref, kaf_ref), (lfb_ref, kab_ref))):
        z = seg(1 + d)
        e = jnp.exp(-jnp.abs(z))
        big = 1.0 / (1.0 + e)
        small = e * big
        pos = z >= 0.0
        sig_neg = jnp.where(pos, small, big)
        _store_heads(lf_ref, jnp.log(jnp.where(pos, big, small) + lbf_ref[d:d + 1, :] * sig_neg))
        _store_heads(key_ref, omlb_ref[d:d + 1, :] * sig_neg)
    _store_heads(ia_ref, seg(3))
    _store_heads(ga_ref, seg(4))
    qb = seg(5) * RB_DK ** -0.5
    kb = seg(6)
    if use_rope:
        qb = _rope_heads(qb, cos_ref[...], sin_ref[...], RB_HEADS)
        kb = _rope_heads(kb, cos_ref[...], sin_ref[...], RB_HEADS)
    _store_heads(qb_ref, qb)
    _store_heads(kb_ref, kb)
    _store_heads(vb_ref, seg(7))
    _store_heads(gb_ref, seg(8))


def _row_tiled_call(body, name, x, sc, sh, consts, seq_inputs, n_heads, out_widths, out_dtypes):
    b, l, d = x.shape
    tm = min(l, PROJ_ROWS)
    assert l % tm == 0

    def mod_spec(m):
        if m.shape[0] == b:
            return pl.BlockSpec((None, 1, d), lambda bi, ri: (bi, 0, 0))
        return pl.BlockSpec((None, 1, d), lambda bi, ri: (0, 0, 0))

    def const_spec(a):
        return pl.BlockSpec(a.shape, lambda bi, ri, nd=a.ndim: (0,) * nd)

    return pl.pallas_call(
        body,
        grid=(b, l // tm),
        in_specs=[pl.BlockSpec((None, tm, d), lambda bi, ri: (bi, ri, 0)), mod_spec(sc), mod_spec(sh)]
        + [const_spec(a) for a in consts]
        + [pl.BlockSpec((tm, a.shape[-1]), lambda bi, ri: (ri, 0)) for a in seq_inputs],
        out_specs=[pl.BlockSpec((None, n_heads, tm, w // n_heads), lambda bi, ri: (bi, 0, ri, 0))
                   for w in out_widths],
        out_shape=[jax.ShapeDtypeStruct((b, n_heads, l, w // n_heads), dt)
                   for w, dt in zip(out_widths, out_dtypes)],
        compiler_params=pltpu.CompilerParams(
            dimension_semantics=("parallel", "parallel"),
            vmem_limit_bytes=VMEM_LIMIT_BYTES),
        name=name,
    )(x, sc, sh, *consts, *seq_inputs)


def _even_mixer(x, sc, sh, w_in, lb, ha_norm_w, rb_log_gamma, rope, states):
    f32, bf = jnp.float32, jnp.bfloat16
    assert HA_HEADS == RB_HEADS
    consts = [w_in.astype(bf), jnp.maximum(lb, LB_FLOOR), 1.0 - lb]
    seq = []
    if rope is not None:
        cos, sin = rope
        seq = [jnp.concatenate([cos, cos], axis=-1), jnp.concatenate([-sin, sin], axis=-1)]
    qa, ka_f, ka_b, lf_f, lf_b, ia, ga, qb, kb, vb, gb = _row_tiled_call(
        functools.partial(_even_in_body, use_rope=rope is not None), "even_in_proj", x, sc, sh, consts, seq,
        HA_HEADS, (HA_KEY,) * 5 + (HA_VAL, HA_VAL, RB_KEY, RB_KEY, RB_VAL, RB_VAL),
        (f32,) * 5 + (bf, f32, f32, f32, bf, f32))
    ya, sa_f, sa_b = _bidir_scan(qa, ka_f, ka_b, ia, lf_f, lf_b, ga, ha_norm_w, states[0][0], states[0][1],
                                 HA_HEADS)
    yb, sb_f, sb_b = _bidir_scan(qb, kb, kb, vb, None, None, gb, jnp.ones((RB_DV,), f32),
                                 states[1][0], states[1][1], RB_HEADS, log_decay=rb_log_gamma)
    return (ya, yb), ((sa_f, sa_b), (sb_f, sb_b))


def _odd_in_body(x_ref, sc_ref, sh_ref, w_ref, w2_ref, b2_ref, q_ref, k_ref, v_ref, g_ref, lff_ref, lfb_ref):
    h = _modulated_bf16(x_ref, sc_ref, sh_ref)
    offs = np.cumsum((0,) + ODD_SIZES[:4]).tolist()

    def seg(lo, hi):
        return jnp.dot(h, w_ref[:, lo:hi], preferred_element_type=jnp.float32)

    _store_heads(q_ref, seg(offs[0], offs[1]) * GC_DK ** -0.5)
    _store_heads(k_ref, seg(offs[1], offs[2]))
    _store_heads(v_ref, seg(offs[2], offs[3]))
    _store_heads(g_ref, seg(offs[3], offs[4]))
    lr = seg(offs[4], w_ref.shape[-1]).astype(jnp.bfloat16)
    for d, lf_ref in enumerate((lff_ref, lfb_ref)):
        zz = jnp.dot(lr, w2_ref[d], preferred_element_type=jnp.float32) + b2_ref[d:d + 1, :]
        _store_heads(lf_ref, _log_sigmoid(zz) / GC_TAU)


def _odd_mixer(x, sc, sh, w_in, w2, b2, norm_w, states):
    f32, bf = jnp.float32, jnp.bfloat16
    lane = 128
    pad = lane - 2 * GC_RANK
    w_pad = jnp.pad(w_in, ((0, 0), (0, pad))).astype(bf)
    w2_pad = jnp.stack([jnp.pad(w2[0], ((0, lane - GC_RANK), (0, 0))),
                        jnp.pad(w2[1], ((GC_RANK, lane - 2 * GC_RANK), (0, 0)))]).astype(bf)
    q, k, v, g, lf_f, lf_b = _row_tiled_call(
        _odd_in_body, "odd_in_proj", x, sc, sh, [w_pad, w2_pad, b2], [], GC_HEADS,
        (GC_KEY, GC_KEY, GC_VAL, GC_VAL, GC_KEY, GC_KEY), (f32, f32, bf, f32, f32, f32))
    y, s_f, s_b = _bidir_scan(q, k, k, v, lf_f, lf_b, g, norm_w, states[0], states[1], GC_HEADS)
    return (y,), (s_f, s_b)


def _mixer_out_body(*refs, n_y):
    y_refs = refs[:n_y]
    (wo_ref, x_ref, g1_ref, lnw_ref, lnb_ref, sc2_ref, sh2_ref, wr_ref, x1_ref, h2_ref, aff_ref) = refs[n_y:]
    mix = None
    lo = 0
    for y_ref in y_refs:
        hi = lo + y_ref.shape[-1]
        part = jnp.dot(y_ref[...], wo_ref[lo:hi, :], preferred_element_type=jnp.float32)
        mix = part if mix is None else mix + part
        lo = hi
    z = DN_ALPHA * x_ref[...] + g1_ref[...] * mix
    mu = jnp.mean(z, axis=-1, keepdims=True)
    zc = z - mu
    var = jnp.mean(zc * zc, axis=-1, keepdims=True)
    x1 = zc * lax.rsqrt(var + LN_EPS) * lnw_ref[...] + lnb_ref[...]
    x1_ref[...] = x1
    h2 = (x1 * (1.0 + sc2_ref[...]) + sh2_ref[...]).astype(jnp.bfloat16)
    h2_ref[...] = h2
    logits = lax.dot_general(wr_ref[...], h2, _NT, preferred_element_type=jnp.float32)
    ex = jnp.exp(logits - jnp.max(logits, axis=0, keepdims=True))
    aff_ref[...] = ex / jnp.sum(ex, axis=0, keepdims=True)


def _mixer_out(ys, w_out, x, g1, ln_w, ln_b, sc2, sh2, w_router):
    b, l, d = x.shape
    e = w_router.shape[-1]
    tm = min(l, MIXER_OUT_ROWS)
    assert l % tm == 0

    def mod_spec(m):
        if m.shape[0] == b:
            return pl.BlockSpec((None, 1, d), lambda bi, ri: (bi, 0, 0))
        return pl.BlockSpec((None, 1, d), lambda bi, ri: (0, 0, 0))

    def const_spec(a):
        return pl.BlockSpec(a.shape, lambda bi, ri, nd=a.ndim: (0,) * nd)

    def rows(w):
        return pl.BlockSpec((None, tm, w), lambda bi, ri: (bi, ri, 0))

    consts_a = [w_out.astype(jnp.bfloat16)]
    consts_b = [ln_w.reshape(1, d), ln_b.reshape(1, d)]
    wr_t = w_router.T.astype(jnp.bfloat16)
    return pl.pallas_call(
        functools.partial(_mixer_out_body, n_y=len(ys)),
        grid=(b, l // tm),
        in_specs=[rows(y.shape[-1]) for y in ys] + [const_spec(consts_a[0]), rows(d), mod_spec(g1)]
        + [const_spec(a) for a in consts_b] + [mod_spec(sc2), mod_spec(sh2), const_spec(wr_t)],
        out_specs=[rows(d), rows(d), pl.BlockSpec((None, e, tm), lambda bi, ri: (bi, 0, ri))],
        out_shape=[jax.ShapeDtypeStruct((b, l, d), jnp.float32), jax.ShapeDtypeStruct((b, l, d), jnp.bfloat16),
                   jax.ShapeDtypeStruct((b, e, l), jnp.float32)],
        compiler_params=pltpu.CompilerParams(
            dimension_semantics=("parallel", "parallel"),
            vmem_limit_bytes=VMEM_LIMIT_BYTES),
        name="mixer_out",
    )(*ys, *consts_a, x, g1, *consts_b, sc2, sh2, wr_t)


def _expert_ffn_body(*refs, cast_next):
    if cast_next:
        (x_ref, gate_ref, wg_ref, wu_ref, wd_ref, ng_ref, nu_ref, nd_ref,
         out_ref, og_ref, ou_ref, od_ref, hid_ref) = refs
        og_ref[...] = ng_ref[...].astype(og_ref.dtype)
        ou_ref[...] = nu_ref[...].astype(ou_ref.dtype)
        od_ref[...] = nd_ref[...].astype(od_ref.dtype)
    else:
        x_ref, gate_ref, wg_ref, wu_ref, wd_ref, out_ref, hid_ref = refs
    tb, cap, d = x_ref.shape
    ff = wg_ref.shape[-1]
    x = x_ref[...].reshape(tb * cap, d)
    for j in range(ff // FFN_FF_TILE):
        cols = slice(j * FFN_FF_TILE, (j + 1) * FFN_FF_TILE)
        g = jnp.dot(x, wg_ref[:, cols], preferred_element_type=jnp.float32)
        u = jnp.dot(x, wu_ref[:, cols], preferred_element_type=jnp.float32)
        hid_ref[:, cols] = (g * jax.nn.sigmoid(g) * u).astype(jnp.bfloat16)
    y = jnp.dot(hid_ref[...], wd_ref[...], preferred_element_type=jnp.float32)
    out_ref[...] = (y.reshape(tb, cap, d) * gate_ref[...]).astype(out_ref.dtype)


def _expert_ffn(xs, gate, w_gate, w_up, w_down, next_layer=None, next_weights=None):
    b, e, cap, d = xs.shape
    ff = w_gate.shape[-1]
    tb = max(1, min(b, FFN_ROWS_PER_STEP // cap))
    steps = b // tb
    assert b % tb == 0 and ff % FFN_FF_TILE == 0
    cast_next = next_weights is not None
    in_specs = [
        pl.BlockSpec((tb, None, cap, d), lambda ei, bi: (bi, ei, 0, 0)),
        pl.BlockSpec((tb, None, cap, 1), lambda ei, bi: (bi, ei, 0, 0)),
        pl.BlockSpec((None, d, ff), lambda ei, bi: (ei, 0, 0)),
        pl.BlockSpec((None, d, ff), lambda ei, bi: (ei, 0, 0)),
        pl.BlockSpec((None, ff, d), lambda ei, bi: (ei, 0, 0)),
    ]
    out_specs = [pl.BlockSpec((tb, None, cap, d), lambda ei, bi: (bi, ei, 0, 0))]
    out_shape = [jax.ShapeDtypeStruct((b, e, cap, d), jnp.bfloat16)]
    args = [xs, gate, w_gate, w_up, w_down]
    if cast_next:
        d_slab, f_slab = d // steps, ff // steps
        assert d % steps == 0 and ff % steps == 0 and d_slab % BF16_SUBLANES == 0 and f_slab % BF16_SUBLANES == 0
        in_specs += [
            pl.BlockSpec((None, None, d_slab, ff), lambda ei, bi: (next_layer, ei, bi, 0)),
            pl.BlockSpec((None, None, d_slab, ff), lambda ei, bi: (next_layer, ei, bi, 0)),
            pl.BlockSpec((None, None, f_slab, d), lambda ei, bi: (next_layer, ei, bi, 0)),
        ]
        out_specs += [
            pl.BlockSpec((None, d_slab, ff), lambda ei, bi: (ei, bi, 0)),
            pl.BlockSpec((None, d_slab, ff), lambda ei, bi: (ei, bi, 0)),
            pl.BlockSpec((None, f_slab, d), lambda ei, bi: (ei, bi, 0)),
        ]
        out_shape += [jax.ShapeDtypeStruct((e, d, ff), jnp.bfloat16), jax.ShapeDtypeStruct((e, d, ff), jnp.bfloat16),
                      jax.ShapeDtypeStruct((e, ff, d), jnp.bfloat16)]
        args += list(next_weights)
    res = pl.pallas_call(
        functools.partial(_expert_ffn_body, cast_next=cast_next),
        grid=(e, steps),
        in_specs=in_specs,
        out_specs=out_specs,
        out_shape=out_shape,
        scratch_shapes=[pltpu.VMEM((tb * cap, ff), jnp.bfloat16)],
        compiler_params=pltpu.CompilerParams(
            dimension_semantics=("parallel", "parallel"),
            vmem_limit_bytes=VMEM_LIMIT_BYTES),
        name="expert_ffn",
    )(*args)
    return (res[0], tuple(res[1:])) if cast_next else (res[0], None)


def _layer_norm_rows(z, w, b):
    mu = jnp.mean(z, axis=-1, keepdims=True)
    zc = z - mu
    var = jnp.mean(zc * zc, axis=-1, keepdims=True)
    return zc * lax.rsqrt(var + LN_EPS) * w + b


def _combine_ln_body(lohi_ref, idx_ref, idx_all_ref, ys_ref, x_ref, g_ref, w_ref, b_ref, out_ref, ffn_ref, *,
                     window):
    tr, d = x_ref.shape
    e, cap, _ = ys_ref.shape
    bi, ri = pl.program_id(0), pl.program_id(1)
    n_bounds = pl.num_programs(1) + 1
    tokens = lax.broadcasted_iota(jnp.int32, (tr, 1), 0) + ri * tr

    def all_slots():
        onehot = jnp.where(tokens == idx_all_ref[...], 1.0, 0.0).astype(jnp.bfloat16)
        return jnp.dot(onehot, ys_ref[...].reshape(e * cap, d), preferred_element_type=jnp.float32)

    def finish(ffn):
        z = DN_ALPHA * x_ref[...] + g_ref[...] * ffn
        out_ref[...] = _layer_norm_rows(z, w_ref[...], b_ref[...])

    if window == cap:
        finish(all_slots())
        return
    per_dot = CONTRACT_DEPTH // window
    ffn = jnp.zeros((tr, d), jnp.float32)
    overflow = jnp.int32(0)
    for e0 in range(0, e, per_dot):
        wins, rows = [], []
        for ei in range(e0, e0 + per_dot):
            lo = lohi_ref[bi, ei * n_bounds + ri]
            hi = lohi_ref[bi, ei * n_bounds + ri + 1]
            start = jnp.minimum((lo // BF16_SUBLANES) * BF16_SUBLANES, cap - window)
            overflow = overflow + (hi > start + window).astype(jnp.int32)
            wins.append(pltpu.roll(idx_ref[ei], (cap - start) % cap, axis=1)[:, :window])
            rows.append(ys_ref[ei, pl.ds(pl.multiple_of(start, BF16_SUBLANES), window), :])
        onehot = jnp.where(tokens == jnp.concatenate(wins, axis=1), 1.0, 0.0).astype(jnp.bfloat16)
        ffn = ffn + jnp.dot(onehot, jnp.concatenate(rows, axis=0), preferred_element_type=jnp.float32)
    ffn_ref[...] = ffn

    @pl.when(overflow > 0)
    def _():
        ffn_ref[...] = all_slots()

    finish(ffn_ref[...])


def _combine_ln(ys, idx, x, g, w, b):
    bsz, e, cap, d = ys.shape
    n = x.shape[1]
    tr = min(n, COMBINE_ROWS)
    window = min(cap, COMBINE_WINDOW)
    assert n % tr == 0 and CONTRACT_DEPTH % window == 0 and e % (CONTRACT_DEPTH // window) == 0
    assert cap % BF16_SUBLANES == 0
    per_sample = g.shape[0] == bsz
    bounds = jnp.arange(0, n + 1, tr, dtype=jnp.int32)
    lohi = jnp.sum(idx[..., None] < bounds, axis=2, dtype=jnp.int32).reshape(bsz, e * (n // tr + 1))
    grid_spec = pltpu.PrefetchScalarGridSpec(
        num_scalar_prefetch=1,
        grid=(bsz, n // tr),
        in_specs=[
            pl.BlockSpec((None, e, 1, cap), lambda bi, ri, lohi: (bi, 0, 0, 0)),
            pl.BlockSpec((None, 1, e * cap), lambda bi, ri, lohi: (bi, 0, 0)),
            pl.BlockSpec((None, e, cap, d), lambda bi, ri, lohi: (bi, 0, 0, 0)),
            pl.BlockSpec((None, tr, d), lambda bi, ri, lohi: (bi, ri, 0)),
            pl.BlockSpec((None, 1, d), (lambda bi, ri, lohi: (bi, 0, 0)) if per_sample
                         else (lambda bi, ri, lohi: (0, 0, 0))),
            pl.BlockSpec((1, d), lambda bi, ri, lohi: (0, 0)),
            pl.BlockSpec((1, d), lambda bi, ri, lohi: (0, 0)),
        ],
        out_specs=pl.BlockSpec((None, tr, d), lambda bi, ri, lohi: (bi, ri, 0)),
        scratch_shapes=[pltpu.VMEM((tr, d), jnp.float32)],
    )
    return pl.pallas_call(
        functools.partial(_combine_ln_body, window=window),
        grid_spec=grid_spec,
        out_shape=jax.ShapeDtypeStruct((bsz, n, d), jnp.float32),
        compiler_params=pltpu.CompilerParams(
            dimension_semantics=("parallel", "parallel"),
            vmem_limit_bytes=VMEM_LIMIT_BYTES),
        name="combine_ln",
    )(lohi, idx.reshape(bsz, e, 1, cap), idx.reshape(bsz, 1, e * cap), ys, x, g, w.reshape(1, d), b.reshape(1, d))


def _ec_ffn_ln(x, h, aff, g, experts, ln_w, ln_b, next_layer=None, next_weights=None):
    n = x.shape[1]
    cap = EC_CAPACITY_FACTOR * n // N_EXPERTS
    gate, idx = lax.top_k(aff, cap)
    idx, gate = lax.sort((idx, gate), dimension=2, num_keys=1)
    xs = jax.vmap(lambda hb, ib: hb[ib])(h, idx)
    ys, next_experts = _expert_ffn(xs, gate[..., None], *experts, next_layer=next_layer, next_weights=next_weights)
    return _combine_ln(ys, idx, x, g, ln_w, ln_b), next_experts


def kernel(x, c, ctx, c_ctx, ada_w, ada_b, ln_w, ln_b, even_w_in, even_w_out, ha_lb, ha_norm,
           rb_decay, odd_w_in, odd_w_out, gc_w2, gc_b2, gc_norm, router_w, exp_w_gate, exp_w_up,
           exp_w_down):
    n_lat = x.shape[1]
    rows = n_lat // GRID_W
    rope = _axial_rope(rows)
    b_ctx = ctx.shape[0]
    lb_p = jax.nn.softmax(ha_lb, axis=0)
    lb_all = jnp.cumsum(lb_p, axis=0) - lb_p[0]
    cond_lat = jax.nn.silu(c)
    cond_ctx = jax.nn.silu(c_ctx)
    stacked_experts = (exp_w_gate, exp_w_up, exp_w_down)
    experts = tuple(w[0].astype(jnp.bfloat16) for w in stacked_experts)

    for l in range(DEPTH):
        last = l == DEPTH - 1
        mod_lat = (cond_lat @ ada_w[l] + ada_b[l])[:, None, :]
        mod_ctx = (cond_ctx @ ada_w[l] + ada_b[l])[None, None, :]
        sh1, sc1, g1, sh2, sc2, g2 = jnp.split(mod_lat, 6, axis=-1)
        csh1, csc1, cg1, csh2, csc2, cg2 = jnp.split(mod_ctx, 6, axis=-1)

        j = l // 2
        if l % 2 == 0:
            z_a = jnp.zeros((b_ctx, HA_HEADS, HA_DV, HA_DK), jnp.float32)
            z_b = jnp.zeros((b_ctx, RB_HEADS, RB_DV, RB_DK), jnp.float32)
            log_gamma = jax.nn.log_sigmoid(rb_decay[j])
            w_out = even_w_out[j]
            y_ctx, st = _even_mixer(ctx, csc1, csh1, even_w_in[j], lb_all[j], ha_norm[j], log_gamma, None,
                                    ((z_a, z_a), (z_b, z_b)))
            y_lat, _ = _even_mixer(x, sc1, sh1, even_w_in[j], lb_all[j], ha_norm[j], log_gamma, rope, st)
        else:
            z_c = jnp.zeros((b_ctx, GC_HEADS, GC_DV, GC_DK), jnp.float32)
            w_out = odd_w_out[j]
            y_ctx, st = _odd_mixer(ctx, csc1, csh1, odd_w_in[j], gc_w2[j], gc_b2[j], gc_norm[j], (z_c, z_c))
            y_lat, _ = _odd_mixer(x, sc1, sh1, odd_w_in[j], gc_w2[j], gc_b2[j], gc_norm[j], st)

        x, h2, aff = _mixer_out(y_lat, w_out, x, g1, ln_w[l, 0], ln_b[l, 0], sc2, sh2, router_w[l])
        x, next_experts = _ec_ffn_ln(x, h2, aff, g2, experts, ln_w[l, 1], ln_b[l, 1],
                                     next_layer=None if last else l + 1,
                                     next_weights=None if last else stacked_experts)
        if not last:
            ctx, h2, aff = _mixer_out(y_ctx, w_out, ctx, cg1, ln_w[l, 0], ln_b[l, 0], csc2, csh2, router_w[l])
            ctx, _ = _ec_ffn_ln(ctx, h2, aff, cg2, experts, ln_w[l, 1], ln_b[l, 1])
            experts = next_experts
    return x
```

```python
import functools

import jax
import jax.numpy as jnp
import numpy as np
from jax import lax
from jax.experimental import pallas as pl
from jax.experimental.pallas import tpu as pltpu

D_MODEL = 1024
DEPTH = 4
GRID_W = 64

HA_HEADS = 4
HA_DK = 128
HA_DV = 128
RB_HEADS = 4
RB_DK = 128
RB_DV = 128
GC_HEADS = 4
GC_DK = 128
GC_DV = 256
GC_RANK = 16
GC_TAU = 16.0
N_EXPERTS = 16
EXPERT_FF = 2816
EC_CAPACITY_FACTOR = 2

CHUNK = 64
RETENTION_CHUNK = 128
SUB = 16
N_SUB = CHUNK // SUB
ROPE_BASE = 10000.0
LN_EPS = 1e-5
RMS_EPS = 1e-6
LB_FLOOR = 1e-30
DN_ALPHA = (2 * DEPTH) ** 0.25
LOG2_E = 1.4426950408889634

HA_KEY = HA_HEADS * HA_DK
HA_VAL = HA_HEADS * HA_DV
RB_KEY = RB_HEADS * RB_DK
RB_VAL = RB_HEADS * RB_DV
GC_KEY = GC_HEADS * GC_DK
GC_VAL = GC_HEADS * GC_DV
EVEN_SIZES = (HA_KEY, HA_KEY, HA_KEY, HA_VAL, HA_VAL, RB_KEY, RB_KEY, RB_VAL, RB_VAL)
ODD_SIZES = (GC_KEY, GC_KEY, GC_VAL, GC_VAL, 2 * GC_RANK)

VMEM_LIMIT_BYTES = 56 * 1024 * 1024
FFN_ROWS_PER_STEP = 512
FFN_FF_TILE = 256
SCAN_EPILOGUE_ROWS = 256
COMBINE_ROWS = 256
COMBINE_WINDOW = 64
CONTRACT_DEPTH = 256
BF16_SUBLANES = 16
PROJ_ROWS = 512
MIXER_OUT_ROWS = 1024
SCAN_GROUP = 16

_NT = (((1,), (1,)), ((), ()))
_TN = (((0,), (0,)), ((), ()))


def _axial_rope(rows):
    r_idx, c_idx = jnp.meshgrid(jnp.arange(rows), jnp.arange(GRID_W), indexing='ij')
    n_freq = RB_DK // 4
    freq = ROPE_BASE ** (-jnp.arange(n_freq, dtype=jnp.float32) / n_freq)
    ang = jnp.concatenate([r_idx.reshape(-1, 1).astype(jnp.float32) * freq,
                           c_idx.reshape(-1, 1).astype(jnp.float32) * freq], axis=-1)
    return jnp.cos(ang), jnp.sin(ang)


def _chunk_running_sum(lf, rev):
    c, dk = lf.shape
    row = lax.broadcasted_iota(jnp.int32, (c, dk), 0)
    x = lf
    shift = 1
    while shift < c:
        if rev:
            x = x + jnp.where(row < c - shift, pltpu.roll(x, c - shift, axis=0), 0.0)
        else:
            x = x + jnp.where(row >= shift, pltpu.roll(x, shift, axis=0), 0.0)
        shift *= 2
    return x


def _chunk_operands(q, k, cs, rev):
    bf = jnp.bfloat16
    dk = q.shape[-1]
    cs = cs * LOG2_E
    zero_row = jnp.zeros((1, dk), jnp.float32)
    if not rev:
        ends = [cs[SUB * a + SUB - 1:SUB * a + SUB, :] for a in range(N_SUB)]
        starts = [zero_row] + ends[:-1]
        mids = [cs[SUB * a + SUB // 2 - 1:SUB * a + SUB // 2, :] for a in range(N_SUB)]
        order = list(range(N_SUB))
    else:
        ends = [cs[SUB * a:SUB * a + 1, :] for a in range(N_SUB)]
        starts = ends[1:] + [zero_row]
        mids = [cs[SUB * a + SUB // 2:SUB * a + SUB // 2 + 1, :] for a in range(N_SUB)]
        order = list(range(N_SUB - 1, -1, -1))
    last = ends[order[-1]]

    def per_block(rows):
        return jnp.concatenate([jnp.broadcast_to(r, (SUB, dk)) for r in rows], axis=0)

    mid_b, start_b, end_b = per_block(mids), per_block(starts), per_block(ends)
    qd = q * jnp.exp2(cs - mid_b)
    kd = k * jnp.exp2(mid_b - cs)
    q_in = q * jnp.exp2(cs - start_b)
    k_out = k * jnp.exp2(end_b - cs)
    q_state = q_in * jnp.exp2(start_b)
    k_state = k_out * jnp.exp2(last - end_b)

    def blk(t, a):
        return t[SUB * a:SUB * (a + 1), :]

    zeros_blk = jnp.zeros((SUB, dk), jnp.float32)
    q_cat, k_cat = [], []
    for s in range(N_SUB - 1):
        src = order[s]
        q_rows = [zeros_blk] * N_SUB
        k_rows = [zeros_blk] * N_SUB
        k_rows[src] = blk(k_out, src)
        for t in range(s + 1, N_SUB):
            tgt = order[t]
            piece = blk(q_in, tgt)
            if t > s + 1:
                piece = piece * jnp.exp2(jnp.minimum(starts[tgt] - ends[src], 0.0))
            q_rows[tgt] = piece
        q_cat.append(jnp.concatenate(q_rows, axis=0))
        k_cat.append(jnp.concatenate(k_rows, axis=0))
    return dict(qd=qd.astype(bf), kd=kd.astype(bf),
                q_cat=jnp.concatenate(q_cat, axis=1).astype(bf), k_cat=jnp.concatenate(k_cat, axis=1).astype(bf),
                q_state=q_state.astype(bf), k_state=k_state.astype(bf), decay=jnp.exp2(last))


def _bidir_scan_body(*refs, chunk, n_chunks, group, heads, scalar_decay, n_cast):
    if n_cast:
        n_in = len(refs) - 1 - 3 - n_cast
        cast_in = refs[n_in - n_cast:n_in]
        cast_out = refs[n_in + 3:n_in + 3 + n_cast]
        for src, dst in zip(cast_in, cast_out):
            dst[...] = src[...].astype(dst.dtype)
        refs = refs[:n_in - n_cast] + refs[n_in:n_in + 3] + refs[n_in + 3 + n_cast:]
    if scalar_decay:
        (q_ref, kf_ref, kb_ref, v_ref, lg_ref, gate_ref, nw_ref, s0f_ref, s0b_ref,
         y_ref, sf_ref, sb_ref, o_ref) = refs
    else:
        (q_ref, kf_ref, kb_ref, v_ref, lff_ref, lfb_ref, gate_ref, nw_ref, s0f_ref, s0b_ref,
         y_ref, sf_ref, sb_ref, o_ref) = refs
    c = chunk
    bf = jnp.bfloat16
    dk = q_ref.shape[-1]
    dv = o_ref.shape[-1]
    row = lax.broadcasted_iota(jnp.int32, (c, c), 0)
    col = lax.broadcasted_iota(jnp.int32, (c, c), 1)
    same_blk = (row // SUB) == (col // SUB)
    tri = {False: (col <= row), True: (col >= row)}
    dmask = {r: same_blk & t for r, t in tri.items()}
    k_refs = {False: kf_ref, True: kb_ref}
    st_refs = {False: sf_ref, True: sb_ref}
    sf_ref[...] = s0f_ref[...]
    sb_ref[...] = s0b_ref[...]
    if scalar_decay:
        pos = lax.broadcasted_iota(jnp.int32, (c, dk), 0).astype(jnp.float32)
        dist = (row - col).astype(jnp.float32)
        pair_decay, q_edge, k_edge, chunk_decay = {}, {}, {}, {}
        for hd in range(heads):
            for rev in (False, True):
                lg = lg_ref[int(rev), hd, 0:1, :]
                to_entry = (c - pos) if rev else (pos + 1.0)
                to_exit = pos if rev else (c - 1.0 - pos)
                signed = -dist if rev else dist
                pair_decay[hd, rev] = jnp.where(tri[rev], jnp.exp(lg[:, :c] * jnp.maximum(signed, 0.0)), 0.0)
                q_edge[hd, rev] = jnp.exp(lg * to_entry)
                k_edge[hd, rev] = jnp.exp(lg * to_exit)
                chunk_decay[hd, rev] = jnp.exp(lg * float(c))

    def step(n, accumulate):
        units = []
        for hd in range(heads):
            for u in range(group):
                for rev in (False, True):
                    chunk = n * group + u
                    if rev:
                        chunk = n_chunks - 1 - chunk
                    units.append((hd, rev, pl.ds(pl.multiple_of(chunk * c, c), c)))
        vs = [v_ref[hd, rows, :].astype(bf) for hd, _, rows in units]
        if scalar_decay:
            qs = [q_ref[hd, rows, :] for hd, _, rows in units]
            ks = [k_refs[rev][hd, rows, :] for hd, rev, rows in units]
            ops = [dict(q_state=(q * q_edge[hd, rev]).astype(bf), k_state=(k * k_edge[hd, rev]).astype(bf),
                        decay=chunk_decay[hd, rev]) for q, k, (hd, rev, _) in zip(qs, ks, units)]
            qk = [lax.dot_general(q.astype(bf), k.astype(bf), _NT, preferred_element_type=jnp.float32)
                  for q, k in zip(qs, ks)]
            scores = [(s * pair_decay[hd, rev]).astype(bf) for s, (hd, rev, _) in zip(qk, units)]
        else:
            lfs = [(lfb_ref if rev else lff_ref)[hd, rows, :] for hd, rev, rows in units]
            css = [_chunk_running_sum(lf, rev) for lf, (_, rev, _) in zip(lfs, units)]
            ops = [_chunk_operands(q_ref[hd, rows, :], k_refs[rev][hd, rows, :], cs, rev)
                   for cs, (hd, rev, rows) in zip(css, units)]
            diag = [lax.dot_general(p["qd"], p["kd"], _NT, preferred_element_type=jnp.float32) for p in ops]
            cross = [lax.dot_general(p["q_cat"], p["k_cat"], _NT, preferred_element_type=jnp.float32) for p in ops]
            scores = [(jnp.where(dmask[rev], d, 0.0) + x).astype(bf)
                      for d, x, (_, rev, _) in zip(diag, cross, units)]
        kv = [lax.dot_general(v, p["k_state"], _TN, preferred_element_type=jnp.float32) for v, p in zip(vs, ops)]
        intra = [jnp.dot(s, v, preferred_element_type=jnp.float32) for s, v in zip(scores, vs)]
        for hd in range(heads):
            for rev in (False, True):
                st = st_refs[rev][hd]
                for i, (h_i, r, rows) in enumerate(units):
                    if (h_i, r) != (hd, rev):
                        continue
                    o = intra[i] + lax.dot_general(ops[i]["q_state"], st.astype(bf), _NT,
                                                   preferred_element_type=jnp.float32)
                    st = st * ops[i]["decay"] + kv[i]
                    if accumulate:
                        o_ref[hd, rows, :] += o
                    else:
                        o_ref[hd, rows, :] = o
                st_refs[rev][hd] = st

    steps = n_chunks // group

    def first(n, carry):
        step(n, False)
        return carry

    def second(n, carry):
        step(n, True)
        return carry

    lax.fori_loop(0, steps // 2, first, 0)
    lax.fori_loop(steps // 2, steps, second, 0)

    tr = min(SCAN_EPILOGUE_ROWS, n_chunks * c)

    def epilogue(i, carry):
        rows = pl.ds(pl.multiple_of(i * tr, tr), tr)
        for hd in range(heads):
            o = o_ref[hd, rows, :]
            g = gate_ref[hd, rows, :]
            y = o * lax.rsqrt(jnp.mean(o * o, axis=-1, keepdims=True) + RMS_EPS) * nw_ref[...]
            y_ref[rows, hd * dv:(hd + 1) * dv] = (y * (g * jax.nn.sigmoid(g))).astype(y_ref.dtype)
        return carry

    lax.fori_loop(0, n_chunks * c // tr, epilogue, 0)


def _bidir_scan(q, k_f, k_b, v, lf_f, lf_b, gate, norm_w, s0_f, s0_b, n_heads, log_decay=None, cast=None):
    b, _, l, dk = q.shape
    dv = v.shape[-1]
    scalar_decay = log_decay is not None
    chunk = RETENTION_CHUNK if scalar_decay and l % (2 * RETENTION_CHUNK) == 0 else CHUNK
    n_chunks = l // chunk
    group = min(SCAN_GROUP, n_chunks // 2)
    heads = min(n_heads, SCAN_GROUP // group)
    assert l % chunk == 0 and n_chunks % (2 * group) == 0 and n_heads % heads == 0 and chunk <= dk
    seq_k = pl.BlockSpec((None, heads, l, dk), lambda bi, hi: (bi, hi, 0, 0))
    seq_v = pl.BlockSpec((None, heads, l, dv), lambda bi, hi: (bi, hi, 0, 0))
    out_y = pl.BlockSpec((None, l, heads * dv), lambda bi, hi: (bi, 0, hi))
    state = pl.BlockSpec((None, heads, dv, dk), lambda bi, hi: (bi, hi, 0, 0))
    if scalar_decay:
        lg = jnp.broadcast_to(log_decay.astype(jnp.float32)[:, :, None, None], (2, n_heads, 8, dk))
        decay_args = (lg,)
        decay_specs = [pl.BlockSpec((2, heads, 8, dk), lambda bi, hi: (0, hi, 0, 0))]
    else:
        decay_args = (lf_f, lf_b)
        decay_specs = [seq_k, seq_k]
    h_steps = n_heads // heads
    cast_args, cast_in_specs, cast_out_specs, cast_out_shape = [], [], [], []
    plain_cast = None
    if cast is not None:
        layer, tensors = cast
        steps = b * h_steps

        def fits(w):
            _, e, r, _ = w.shape
            return steps % e == 0 and r % (steps // e) == 0 and (r // (steps // e)) % BF16_SUBLANES == 0

        if not all(fits(w) for w in tensors):
            plain_cast = tuple(w[layer].astype(jnp.bfloat16) for w in tensors)
            tensors = ()
        for w in tensors:
            _, e, r, c = w.shape
            slabs = steps // e
            cast_args.append(w)
            cast_in_specs.append(pl.BlockSpec(
                (None, None, r // slabs, c),
                lambda bi, hi, slabs=slabs: (layer, (bi * h_steps + hi) // slabs, (bi * h_steps + hi) % slabs, 0)))
            cast_out_specs.append(pl.BlockSpec(
                (None, r // slabs, c),
                lambda bi, hi, slabs=slabs: ((bi * h_steps + hi) // slabs, (bi * h_steps + hi) % slabs, 0)))
            cast_out_shape.append(jax.ShapeDtypeStruct((e, r, c), jnp.bfloat16))
    res = pl.pallas_call(
        functools.partial(_bidir_scan_body, chunk=chunk, n_chunks=n_chunks, group=group, heads=heads,
                          scalar_decay=scalar_decay, n_cast=len(cast_args)),
        grid=(b, h_steps),
        in_specs=[seq_k, seq_k, seq_k, seq_v] + decay_specs + [
            seq_v, pl.BlockSpec((1, dv), lambda bi, hi: (0, 0)), state, state] + cast_in_specs,
        out_specs=[out_y, state, state] + cast_out_specs,
        out_shape=[jax.ShapeDtypeStruct((b, l, n_heads * dv), jnp.bfloat16),
                   jax.ShapeDtypeStruct((b, n_heads, dv, dk), jnp.float32),
                   jax.ShapeDtypeStruct((b, n_heads, dv, dk), jnp.float32)] + cast_out_shape,
        scratch_shapes=[pltpu.VMEM((heads, l, dv), jnp.float32)],
        compiler_params=pltpu.CompilerParams(
            dimension_semantics=("parallel", "parallel"),
            vmem_limit_bytes=VMEM_LIMIT_BYTES),
        name="bidir_scan",
    )(q, k_f, k_b, v, *decay_args, gate, norm_w.astype(jnp.float32).reshape(1, dv), s0_f, s0_b, *cast_args)
    if cast is None:
        return tuple(res)
    return (*res[:3], plain_cast if plain_cast is not None else tuple(res[3:]))


def _log_sigmoid(x):
    return jnp.minimum(x, 0.0) - jnp.log(1.0 + jnp.exp(-jnp.abs(x)))


def _store_heads(ref, t):
    n_heads, _, dh = ref.shape
    for hh in range(n_heads):
        ref[hh] = t[:, hh * dh:(hh + 1) * dh].astype(ref.dtype)


def _modulated_bf16(x_ref, sc_ref, sh_ref):
    return (x_ref[...] * (1.0 + sc_ref[...]) + sh_ref[...]).astype(jnp.bfloat16)


def _rope_heads(t, cos2, sin2, n_heads):
    dk = t.shape[-1] // n_heads
    out = []
    for hh in range(n_heads):
        th = t[:, hh * dk:(hh + 1) * dk]
        out.append(th * cos2 + pltpu.roll(th, dk // 2, axis=1) * sin2)
    return jnp.concatenate(out, axis=-1)


def _even_in_body(*refs, use_rope):
    if use_rope:
        x_ref, sc_ref, sh_ref, w_ref, lbf_ref, omlb_ref, cos_ref, sin_ref = refs[:8]
    else:
        x_ref, sc_ref, sh_ref, w_ref, lbf_ref, omlb_ref = refs[:6]
    (qa_ref, kaf_ref, kab_ref, lff_ref, lfb_ref, ia_ref, ga_ref, qb_ref, kb_ref, vb_ref, gb_ref) = refs[-11:]
    h = _modulated_bf16(x_ref, sc_ref, sh_ref)
    offs = np.cumsum((0,) + EVEN_SIZES).tolist()

    def seg(i):
        return jnp.dot(h, w_ref[:, offs[i]:offs[i + 1]], preferred_element_type=jnp.float32)

    _store_heads(qa_ref, seg(0) * HA_DK ** -0.5)
    for d, (lf_ref, key_ref) in enumerate(((lff_ref, kaf_ref), (lfb_ref, kab_ref))):
        z = seg(1 + d)
        e = jnp.exp(-jnp.abs(z))
        big = 1.0 / (1.0 + e)
        small = e * big
        pos = z >= 0.0
        sig_neg = jnp.where(pos, small, big)
        _store_heads(lf_ref, jnp.log(jnp.where(pos, big, small) + lbf_ref[d:d + 1, :] * sig_neg))
        _store_heads(key_ref, omlb_ref[d:d + 1, :] * sig_neg)
    _store_heads(ia_ref, seg(3))
    _store_heads(ga_ref, seg(4))
    qb = seg(5) * RB_DK ** -0.5
    kb = seg(6)
    if use_rope:
        qb = _rope_heads(qb, cos_ref[...], sin_ref[...], RB_HEADS)
        kb = _rope_heads(kb, cos_ref[...], sin_ref[...], RB_HEADS)
    _store_heads(qb_ref, qb)
    _store_heads(kb_ref, kb)
    _store_heads(vb_ref, seg(7))
    _store_heads(gb_ref, seg(8))


def _row_tiled_call(body, name, x, sc, sh, consts, seq_inputs, n_heads, out_widths, out_dtypes):
    b, l, d = x.shape
    tm = min(l, PROJ_ROWS)
    assert l % tm == 0

    def mod_spec(m):
        if m.shape[0] == b:
            return pl.BlockSpec((None, 1, d), lambda bi, ri: (bi, 0, 0))
        return pl.BlockSpec((None, 1, d), lambda bi, ri: (0, 0, 0))

    def const_spec(a):
        return pl.BlockSpec(a.shape, lambda bi, ri, nd=a.ndim: (0,) * nd)

    return pl.pallas_call(
        body,
        grid=(b, l // tm),
        in_specs=[pl.BlockSpec((None, tm, d), lambda bi, ri: (bi, ri, 0)), mod_spec(sc), mod_spec(sh)]
        + [const_spec(a) for a in consts]
        + [pl.BlockSpec((tm, a.shape[-1]), lambda bi, ri: (ri, 0)) for a in seq_inputs],
        out_specs=[pl.BlockSpec((None, n_heads, tm, w // n_heads), lambda bi, ri: (bi, 0, ri, 0))
                   for w in out_widths],
        out_shape=[jax.ShapeDtypeStruct((b, n_heads, l, w // n_heads), dt)
                   for w, dt in zip(out_widths, out_dtypes)],
        compiler_params=pltpu.CompilerParams(
            dimension_semantics=("parallel", "parallel"),
            vmem_limit_bytes=VMEM_LIMIT_BYTES),
        name=name,
    )(x, sc, sh, *consts, *seq_inputs)


def _even_mixer(x, sc, sh, w_in, lb, ha_norm_w, rb_log_gamma, rope, states, cast=None):
    f32, bf = jnp.float32, jnp.bfloat16
    assert HA_HEADS == RB_HEADS
    consts = [w_in.astype(bf), jnp.maximum(lb, LB_FLOOR), 1.0 - lb]
    seq = []
    if rope is not None:
        cos, sin = rope
        seq = [jnp.concatenate([cos, cos], axis=-1), jnp.concatenate([-sin, sin], axis=-1)]
    qa, ka_f, ka_b, lf_f, lf_b, ia, ga, qb, kb, vb, gb = _row_tiled_call(
        functools.partial(_even_in_body, use_rope=rope is not None), "even_in_proj", x, sc, sh, consts, seq,
        HA_HEADS, (HA_KEY,) * 5 + (HA_VAL, HA_VAL, RB_KEY, RB_KEY, RB_VAL, RB_VAL),
        (f32,) * 5 + (bf, f32, f32, f32, bf, f32))
    ya, sa_f, sa_b, *casted = _bidir_scan(qa, ka_f, ka_b, ia, lf_f, lf_b, ga, ha_norm_w, states[0][0], states[0][1],
                                          HA_HEADS, cast=cast)
    yb, sb_f, sb_b = _bidir_scan(qb, kb, kb, vb, None, None, gb, jnp.ones((RB_DV,), f32),
                                 states[1][0], states[1][1], RB_HEADS, log_decay=rb_log_gamma)
    return (ya, yb), ((sa_f, sa_b), (sb_f, sb_b)), (casted[0] if casted else None)


def _odd_in_body(x_ref, sc_ref, sh_ref, w_ref, w2_ref, b2_ref, q_ref, k_ref, v_ref, g_ref, lff_ref, lfb_ref):
    h = _modulated_bf16(x_ref, sc_ref, sh_ref)
    offs = np.cumsum((0,) + ODD_SIZES[:4]).tolist()

    def seg(lo, hi):
        return jnp.dot(h, w_ref[:, lo:hi], preferred_element_type=jnp.float32)

    _store_heads(q_ref, seg(offs[0], offs[1]) * GC_DK ** -0.5)
    _store_heads(k_ref, seg(offs[1], offs[2]))
    _store_heads(v_ref, seg(offs[2], offs[3]))
    _store_heads(g_ref, seg(offs[3], offs[4]))
    lr = seg(offs[4], w_ref.shape[-1]).astype(jnp.bfloat16)
    for d, lf_ref in enumerate((lff_ref, lfb_ref)):
        zz = jnp.dot(lr, w2_ref[d], preferred_element_type=jnp.float32) + b2_ref[d:d + 1, :]
        _store_heads(lf_ref, _log_sigmoid(zz) / GC_TAU)


def _odd_mixer(x, sc, sh, w_in, w2, b2, norm_w, states):
    f32, bf = jnp.float32, jnp.bfloat16
    lane = 128
    pad = lane - 2 * GC_RANK
    w_pad = jnp.pad(w_in, ((0, 0), (0, pad))).astype(bf)
    w2_pad = jnp.stack([jnp.pad(w2[0], ((0, lane - GC_RANK), (0, 0))),
                        jnp.pad(w2[1], ((GC_RANK, lane - 2 * GC_RANK), (0, 0)))]).astype(bf)
    q, k, v, g, lf_f, lf_b = _row_tiled_call(
        _odd_in_body, "odd_in_proj", x, sc, sh, [w_pad, w2_pad, b2], [], GC_HEADS,
        (GC_KEY, GC_KEY, GC_VAL, GC_VAL, GC_KEY, GC_KEY), (f32, f32, bf, f32, f32, f32))
    y, s_f, s_b = _bidir_scan(q, k, k, v, lf_f, lf_b, g, norm_w, states[0], states[1], GC_HEADS)
    return (y,), (s_f, s_b)


def _mixer_out_body(*refs, n_y):
    y_refs = refs[:n_y]
    (wo_ref, x_ref, g1_ref, lnw_ref, lnb_ref, sc2_ref, sh2_ref, wr_ref, x1_ref, h2_ref, aff_ref) = refs[n_y:]
    mix = None
    lo = 0
    for y_ref in y_refs:
        hi = lo + y_ref.shape[-1]
        part = jnp.dot(y_ref[...], wo_ref[lo:hi, :], preferred_element_type=jnp.float32)
        mix = part if mix is None else mix + part
        lo = hi
    z = DN_ALPHA * x_ref[...] + g1_ref[...] * mix
    mu = jnp.mean(z, axis=-1, keepdims=True)
    zc = z - mu
    var = jnp.mean(zc * zc, axis=-1, keepdims=True)
    x1 = zc * lax.rsqrt(var + LN_EPS) * lnw_ref[...] + lnb_ref[...]
    x1_ref[...] = x1
    h2 = (x1 * (1.0 + sc2_ref[...]) + sh2_ref[...]).astype(jnp.bfloat16)
    h2_ref[...] = h2
    logits = lax.dot_general(wr_ref[...], h2, _NT, preferred_element_type=jnp.float32)
    ex = jnp.exp(logits - jnp.max(logits, axis=0, keepdims=True))
    aff_ref[...] = ex / jnp.sum(ex, axis=0, keepdims=True)


def _mixer_out(ys, w_out, x, g1, ln_w, ln_b, sc2, sh2, w_router):
    b, l, d = x.shape
    e = w_router.shape[-1]
    tm = min(l, MIXER_OUT_ROWS)
    assert l % tm == 0

    def mod_spec(m):
        if m.shape[0] == b:
            return pl.BlockSpec((None, 1, d), lambda bi, ri: (bi, 0, 0))
        return pl.BlockSpec((None, 1, d), lambda bi, ri: (0, 0, 0))

    def const_spec(a):
        return pl.BlockSpec(a.shape, lambda bi, ri, nd=a.ndim: (0,) * nd)

    def rows(w):
        return pl.BlockSpec((None, tm, w), lambda bi, ri: (bi, ri, 0))

    consts_a = [w_out.astype(jnp.bfloat16)]
    consts_b = [ln_w.reshape(1, d), ln_b.reshape(1, d)]
    wr_t = w_router.T.astype(jnp.bfloat16)
    return pl.pallas_call(
        functools.partial(_mixer_out_body, n_y=len(ys)),
        grid=(b, l // tm),
        in_specs=[rows(y.shape[-1]) for y in ys] + [const_spec(consts_a[0]), rows(d), mod_spec(g1)]
        + [const_spec(a) for a in consts_b] + [mod_spec(sc2), mod_spec(sh2), const_spec(wr_t)],
        out_specs=[rows(d), rows(d), pl.BlockSpec((None, e, tm), lambda bi, ri: (bi, 0, ri))],
        out_shape=[jax.ShapeDtypeStruct((b, l, d), jnp.float32), jax.ShapeDtypeStruct((b, l, d), jnp.bfloat16),
                   jax.ShapeDtypeStruct((b, e, l), jnp.float32)],
        compiler_params=pltpu.CompilerParams(
            dimension_semantics=("parallel", "parallel"),
            vmem_limit_bytes=VMEM_LIMIT_BYTES),
        name="mixer_out",
    )(*ys, *consts_a, x, g1, *consts_b, sc2, sh2, wr_t)


def _expert_ffn_body(*refs, cast_next):
    if cast_next:
        (x_ref, gate_ref, wg_ref, wu_ref, wd_ref, ng_ref, nu_ref, nd_ref,
         out_ref, og_ref, ou_ref, od_ref, hid_ref) = refs
        og_ref[...] = ng_ref[...].astype(og_ref.dtype)
        ou_ref[...] = nu_ref[...].astype(ou_ref.dtype)
        od_ref[...] = nd_ref[...].astype(od_ref.dtype)
    else:
        x_ref, gate_ref, wg_ref, wu_ref, wd_ref, out_ref, hid_ref = refs
    tb, cap, d = x_ref.shape
    ff = wg_ref.shape[-1]
    x = x_ref[...].reshape(tb * cap, d)
    for j in range(ff // FFN_FF_TILE):
        cols = slice(j * FFN_FF_TILE, (j + 1) * FFN_FF_TILE)
        g = jnp.dot(x, wg_ref[:, cols], preferred_element_type=jnp.float32)
        u = jnp.dot(x, wu_ref[:, cols], preferred_element_type=jnp.float32)
        hid_ref[:, cols] = (g * jax.nn.sigmoid(g) * u).astype(jnp.bfloat16)
    y = jnp.dot(hid_ref[...], wd_ref[...], preferred_element_type=jnp.float32)
    out_ref[...] = (y.reshape(tb, cap, d) * gate_ref[...]).astype(out_ref.dtype)


def _expert_ffn(xs, gate, w_gate, w_up, w_down, next_layer=None, next_weights=None):
    b, e, cap, d = xs.shape
    ff = w_gate.shape[-1]
    tb = max(1, min(b, FFN_ROWS_PER_STEP // cap))
    steps = b // tb
    assert b % tb == 0 and ff % FFN_FF_TILE == 0
    cast_next = next_weights is not None
    in_specs = [
        pl.BlockSpec((tb, None, cap, d), lambda ei, bi: (bi, ei, 0, 0)),
        pl.BlockSpec((tb, None, cap, 1), lambda ei, bi: (bi, ei, 0, 0)),
        pl.BlockSpec((None, d, ff), lambda ei, bi: (ei, 0, 0)),
        pl.BlockSpec((None, d, ff), lambda ei, bi: (ei, 0, 0)),
        pl.BlockSpec((None, ff, d), lambda ei, bi: (ei, 0, 0)),
    ]
    out_specs = [pl.BlockSpec((tb, None, cap, d), lambda ei, bi: (bi, ei, 0, 0))]
    out_shape = [jax.ShapeDtypeStruct((b, e, cap, d), jnp.bfloat16)]
    args = [xs, gate, w_gate, w_up, w_down]
    if cast_next:
        d_slab, f_slab = d // steps, ff // steps
        assert d % steps == 0 and ff % steps == 0 and d_slab % BF16_SUBLANES == 0 and f_slab % BF16_SUBLANES == 0
        in_specs += [
            pl.BlockSpec((None, None, d_slab, ff), lambda ei, bi: (next_layer, ei, bi, 0)),
            pl.BlockSpec((None, None, d_slab, ff), lambda ei, bi: (next_layer, ei, bi, 0)),
            pl.BlockSpec((None, None, f_slab, d), lambda ei, bi: (next_layer, ei, bi, 0)),
        ]
        out_specs += [
            pl.BlockSpec((None, d_slab, ff), lambda ei, bi: (ei, bi, 0)),
            pl.BlockSpec((None, d_slab, ff), lambda ei, bi: (ei, bi, 0)),
            pl.BlockSpec((None, f_slab, d), lambda ei, bi: (ei, bi, 0)),
        ]
        out_shape += [jax.ShapeDtypeStruct((e, d, ff), jnp.bfloat16), jax.ShapeDtypeStruct((e, d, ff), jnp.bfloat16),
                      jax.ShapeDtypeStruct((e, ff, d), jnp.bfloat16)]
        args += list(next_weights)
    res = pl.pallas_call(
        functools.partial(_expert_ffn_body, cast_next=cast_next),
        grid=(e, steps),
        in_specs=in_specs,
        out_specs=out_specs,
        out_shape=out_shape,
        scratch_shapes=[pltpu.VMEM((tb * cap, ff), jnp.bfloat16)],
        compiler_params=pltpu.CompilerParams(
            dimension_semantics=("parallel", "parallel"),
            vmem_limit_bytes=VMEM_LIMIT_BYTES),
        name="expert_ffn",
    )(*args)
    return (res[0], tuple(res[1:])) if cast_next else (res[0], None)


def _layer_norm_rows(z, w, b):
    mu = jnp.mean(z, axis=-1, keepdims=True)
    zc = z - mu
    var = jnp.mean(zc * zc, axis=-1, keepdims=True)
    return zc * lax.rsqrt(var + LN_EPS) * w + b


def _combine_ln_body(lohi_ref, idx_ref, idx_all_ref, ys_ref, x_ref, g_ref, w_ref, b_ref, out_ref, ffn_ref, *,
                     window):
    tr, d = x_ref.shape
    e, cap, _ = ys_ref.shape
    bi, ri = pl.program_id(0), pl.program_id(1)
    n_bounds = pl.num_programs(1) + 1
    tokens = lax.broadcasted_iota(jnp.int32, (tr, 1), 0) + ri * tr

    def all_slots():
        onehot = jnp.where(tokens == idx_all_ref[...], 1.0, 0.0).astype(jnp.bfloat16)
        return jnp.dot(onehot, ys_ref[...].reshape(e * cap, d), preferred_element_type=jnp.float32)

    def finish(ffn):
        z = DN_ALPHA * x_ref[...] + g_ref[...] * ffn
        out_ref[...] = _layer_norm_rows(z, w_ref[...], b_ref[...])

    if window == cap:
        finish(all_slots())
        return
    per_dot = CONTRACT_DEPTH // window
    ffn = jnp.zeros((tr, d), jnp.float32)
    overflow = jnp.int32(0)
    for e0 in range(0, e, per_dot):
        wins, rows = [], []
        for ei in range(e0, e0 + per_dot):
            lo = lohi_ref[bi, ei * n_bounds + ri]
            hi = lohi_ref[bi, ei * n_bounds + ri + 1]
            start = jnp.minimum((lo // BF16_SUBLANES) * BF16_SUBLANES, cap - window)
            overflow = overflow + (hi > start + window).astype(jnp.int32)
            wins.append(pltpu.roll(idx_ref[ei], (cap - start) % cap, axis=1)[:, :window])
            rows.append(ys_ref[ei, pl.ds(pl.multiple_of(start, BF16_SUBLANES), window), :])
        onehot = jnp.where(tokens == jnp.concatenate(wins, axis=1), 1.0, 0.0).astype(jnp.bfloat16)
        ffn = ffn + jnp.dot(onehot, jnp.concatenate(rows, axis=0), preferred_element_type=jnp.float32)
    ffn_ref[...] = ffn

    @pl.when(overflow > 0)
    def _():
        ffn_ref[...] = all_slots()

    finish(ffn_ref[...])


def _combine_ln(ys, idx, x, g, w, b):
    bsz, e, cap, d = ys.shape
    n = x.shape[1]
    tr = min(n, COMBINE_ROWS)
    window = min(cap, COMBINE_WINDOW)
    assert n % tr == 0 and CONTRACT_DEPTH % window == 0 and e % (CONTRACT_DEPTH // window) == 0
    assert cap % BF16_SUBLANES == 0
    per_sample = g.shape[0] == bsz
    bounds = jnp.arange(0, n + 1, tr, dtype=jnp.int32)
    lohi = jnp.sum(idx[..., None] < bounds, axis=2, dtype=jnp.int32).reshape(bsz, e * (n // tr + 1))
    grid_spec = pltpu.PrefetchScalarGridSpec(
        num_scalar_prefetch=1,
        grid=(bsz, n // tr),
        in_specs=[
            pl.BlockSpec((None, e, 1, cap), lambda bi, ri, lohi: (bi, 0, 0, 0)),
            pl.BlockSpec((None, 1, e * cap), lambda bi, ri, lohi: (bi, 0, 0)),
            pl.BlockSpec((None, e, cap, d), lambda bi, ri, lohi: (bi, 0, 0, 0)),
            pl.BlockSpec((None, tr, d), lambda bi, ri, lohi: (bi, ri, 0)),
            pl.BlockSpec((None, 1, d), (lambda bi, ri, lohi: (bi, 0, 0)) if per_sample
                         else (lambda bi, ri, lohi: (0, 0, 0))),
            pl.BlockSpec((1, d), lambda bi, ri, lohi: (0, 0)),
            pl.BlockSpec((1, d), lambda bi, ri, lohi: (0, 0)),
        ],
        out_specs=pl.BlockSpec((None, tr, d), lambda bi, ri, lohi: (bi, ri, 0)),
        scratch_shapes=[pltpu.VMEM((tr, d), jnp.float32)],
    )
    return pl.pallas_call(
        functools.partial(_combine_ln_body, window=window),
        grid_spec=grid_spec,
        out_shape=jax.ShapeDtypeStruct((bsz, n, d), jnp.float32),
        compiler_params=pltpu.CompilerParams(
            dimension_semantics=("parallel", "parallel"),
            vmem_limit_bytes=VMEM_LIMIT_BYTES),
        name="combine_ln",
    )(lohi, idx.reshape(bsz, e, 1, cap), idx.reshape(bsz, 1, e * cap), ys, x, g, w.reshape(1, d), b.reshape(1, d))


def _ec_ffn_ln(x, h, aff, g, experts, ln_w, ln_b, next_layer=None, next_weights=None):
    n = x.shape[1]
    cap = EC_CAPACITY_FACTOR * n // N_EXPERTS
    gate, idx = lax.top_k(aff, cap)
    idx, gate = lax.sort((idx, gate), dimension=2, num_keys=1)
    xs = jax.vmap(lambda hb, ib: hb[ib])(h, idx)
    ys, next_experts = _expert_ffn(xs, gate[..., None], *experts, next_layer=next_layer, next_weights=next_weights)
    return _combine_ln(ys, idx, x, g, ln_w, ln_b), next_experts


def kernel(x, c, ctx, c_ctx, ada_w, ada_b, ln_w, ln_b, even_w_in, even_w_out, ha_lb, ha_norm,
           rb_decay, odd_w_in, odd_w_out, gc_w2, gc_b2, gc_norm, router_w, exp_w_gate, exp_w_up,
           exp_w_down):
    n_lat = x.shape[1]
    rows = n_lat // GRID_W
    rope = _axial_rope(rows)
    b_ctx = ctx.shape[0]
    lb_p = jax.nn.softmax(ha_lb, axis=0)
    lb_all = jnp.cumsum(lb_p, axis=0) - lb_p[0]
    cond_lat = jax.nn.silu(c)
    cond_ctx = jax.nn.silu(c_ctx)
    stacked_experts = (exp_w_gate, exp_w_up, exp_w_down)
    experts = None

    for l in range(DEPTH):
        last = l == DEPTH - 1
        mod_lat = (cond_lat @ ada_w[l] + ada_b[l])[:, None, :]
        mod_ctx = (cond_ctx @ ada_w[l] + ada_b[l])[None, None, :]
        sh1, sc1, g1, sh2, sc2, g2 = jnp.split(mod_lat, 6, axis=-1)
        csh1, csc1, cg1, csh2, csc2, cg2 = jnp.split(mod_ctx, 6, axis=-1)

        j = l // 2
        if l % 2 == 0:
            z_a = jnp.zeros((b_ctx, HA_HEADS, HA_DV, HA_DK), jnp.float32)
            z_b = jnp.zeros((b_ctx, RB_HEADS, RB_DV, RB_DK), jnp.float32)
            log_gamma = jax.nn.log_sigmoid(rb_decay[j])
            w_out = even_w_out[j]
            y_ctx, st, _ = _even_mixer(ctx, csc1, csh1, even_w_in[j], lb_all[j], ha_norm[j], log_gamma, None,
                                       ((z_a, z_a), (z_b, z_b)))
            y_lat, _, casted = _even_mixer(x, sc1, sh1, even_w_in[j], lb_all[j], ha_norm[j], log_gamma, rope, st,
                                           cast=(0, stacked_experts) if l == 0 else None)
            if l == 0:
                experts = casted
        else:
            z_c = jnp.zeros((b_ctx, GC_HEADS, GC_DV, GC_DK), jnp.float32)
            w_out = odd_w_out[j]
            y_ctx, st = _odd_mixer(ctx, csc1, csh1, odd_w_in[j], gc_w2[j], gc_b2[j], gc_norm[j], (z_c, z_c))
            y_lat, _ = _odd_mixer(x, sc1, sh1, odd_w_in[j], gc_w2[j], gc_b2[j], gc_norm[j], st)

        x, h2, aff = _mixer_out(y_lat, w_out, x, g1, ln_w[l, 0], ln_b[l, 0], sc2, sh2, router_w[l])
        x, next_experts = _ec_ffn_ln(x, h2, aff, g2, experts, ln_w[l, 1], ln_b[l, 1],
                                     next_layer=None if last else l + 1,
                                     next_weights=None if last else stacked_experts)
        if not last:
            ctx, h2, aff = _mixer_out(y_ctx, w_out, ctx, cg1, ln_w[l, 0], ln_b[l, 0], csc2, csh2, router_w[l])
            ctx, _ = _ec_ffn_ln(ctx, h2, aff, cg2, experts, ln_w[l, 1], ln_b[l, 1])
            experts = next_experts
    return x
```

```python
import functools

import jax
import jax.numpy as jnp
import numpy as np
from jax import lax
from jax.experimental import pallas as pl
from jax.experimental.pallas import tpu as pltpu

D_MODEL = 1024
DEPTH = 4
GRID_W = 64

HA_HEADS = 4
HA_DK = 128
HA_DV = 128
RB_HEADS = 4
RB_DK = 128
RB_DV = 128
GC_HEADS = 4
GC_DK = 128
GC_DV = 256
GC_RANK = 16
GC_TAU = 16.0
N_EXPERTS = 16
EXPERT_FF = 2816
EC_CAPACITY_FACTOR = 2

CHUNK = 64
RETENTION_CHUNK = 128
SUB = 16
N_SUB = CHUNK // SUB
ROPE_BASE = 10000.0
LN_EPS = 1e-5
RMS_EPS = 1e-6
LB_FLOOR = 1e-30
DN_ALPHA = (2 * DEPTH) ** 0.25
LOG2_E = 1.4426950408889634

HA_KEY = HA_HEADS * HA_DK
HA_VAL = HA_HEADS * HA_DV
RB_KEY = RB_HEADS * RB_DK
RB_VAL = RB_HEADS * RB_DV
GC_KEY = GC_HEADS * GC_DK
GC_VAL = GC_HEADS * GC_DV
EVEN_SIZES = (HA_KEY, HA_KEY, HA_KEY, HA_VAL, HA_VAL, RB_KEY, RB_KEY, RB_VAL, RB_VAL)
ODD_SIZES = (GC_KEY, GC_KEY, GC_VAL, GC_VAL, 2 * GC_RANK)

VMEM_LIMIT_BYTES = 56 * 1024 * 1024
FFN_ROWS_PER_STEP = 512
FFN_FF_TILE = 256
SCAN_EPILOGUE_ROWS = 2048
COMBINE_ROWS = 256
COMBINE_WINDOW = 64
CONTRACT_DEPTH = 256
BF16_SUBLANES = 16
PROJ_ROWS = 512
MIXER_OUT_ROWS = 1024
SCAN_GROUP = 16

_NT = (((1,), (1,)), ((), ()))
_TN = (((0,), (0,)), ((), ()))


def _axial_rope(rows):
    r_idx, c_idx = jnp.meshgrid(jnp.arange(rows), jnp.arange(GRID_W), indexing='ij')
    n_freq = RB_DK // 4
    freq = ROPE_BASE ** (-jnp.arange(n_freq, dtype=jnp.float32) / n_freq)
    ang = jnp.concatenate([r_idx.reshape(-1, 1).astype(jnp.float32) * freq,
                           c_idx.reshape(-1, 1).astype(jnp.float32) * freq], axis=-1)
    return jnp.cos(ang), jnp.sin(ang)


def _chunk_running_sum(lf, rev):
    c, dk = lf.shape
    row = lax.broadcasted_iota(jnp.int32, (c, dk), 0)
    x = lf
    shift = 1
    while shift < c:
        if rev:
            x = x + jnp.where(row < c - shift, pltpu.roll(x, c - shift, axis=0), 0.0)
        else:
            x = x + jnp.where(row >= shift, pltpu.roll(x, shift, axis=0), 0.0)
        shift *= 2
    return x


def _chunk_operands(q, k, cs, rev):
    bf = jnp.bfloat16
    dk = q.shape[-1]
    cs = cs * LOG2_E
    zero_row = jnp.zeros((1, dk), jnp.float32)
    if not rev:
        ends = [cs[SUB * a + SUB - 1:SUB * a + SUB, :] for a in range(N_SUB)]
        starts = [zero_row] + ends[:-1]
        mids = [cs[SUB * a + SUB // 2 - 1:SUB * a + SUB // 2, :] for a in range(N_SUB)]
        order = list(range(N_SUB))
    else:
        ends = [cs[SUB * a:SUB * a + 1, :] for a in range(N_SUB)]
        starts = ends[1:] + [zero_row]
        mids = [cs[SUB * a + SUB // 2:SUB * a + SUB // 2 + 1, :] for a in range(N_SUB)]
        order = list(range(N_SUB - 1, -1, -1))
    last = ends[order[-1]]

    def per_block(rows):
        return jnp.concatenate([jnp.broadcast_to(r, (SUB, dk)) for r in rows], axis=0)

    mid_b, start_b, end_b = per_block(mids), per_block(starts), per_block(ends)
    qd = q * jnp.exp2(cs - mid_b)
    kd = k * jnp.exp2(mid_b - cs)
    q_in = q * jnp.exp2(cs - start_b)
    k_out = k * jnp.exp2(end_b - cs)
    q_state = q_in * jnp.exp2(start_b)
    k_state = k_out * jnp.exp2(last - end_b)

    def blk(t, a):
        return t[SUB * a:SUB * (a + 1), :]

    zeros_blk = jnp.zeros((SUB, dk), jnp.float32)
    q_cat, k_cat = [], []
    for s in range(N_SUB - 1):
        src = order[s]
        q_rows = [zeros_blk] * N_SUB
        k_rows = [zeros_blk] * N_SUB
        k_rows[src] = blk(k_out, src)
        for t in range(s + 1, N_SUB):
            tgt = order[t]
            piece = blk(q_in, tgt)
            if t > s + 1:
                piece = piece * jnp.exp2(jnp.minimum(starts[tgt] - ends[src], 0.0))
            q_rows[tgt] = piece
        q_cat.append(jnp.concatenate(q_rows, axis=0))
        k_cat.append(jnp.concatenate(k_rows, axis=0))
    return dict(qd=qd.astype(bf), kd=kd.astype(bf),
                q_cat=jnp.concatenate(q_cat, axis=1).astype(bf), k_cat=jnp.concatenate(k_cat, axis=1).astype(bf),
                q_state=q_state.astype(bf), k_state=k_state.astype(bf), decay=jnp.exp2(last))


def _bidir_scan_body(*refs, chunk, n_chunks, group, heads, scalar_decay, n_cast):
    if n_cast:
        n_in = len(refs) - 1 - 3 - n_cast
        cast_in = refs[n_in - n_cast:n_in]
        cast_out = refs[n_in + 3:n_in + 3 + n_cast]
        for src, dst in zip(cast_in, cast_out):
            dst[...] = src[...].astype(dst.dtype)
        refs = refs[:n_in - n_cast] + refs[n_in:n_in + 3] + refs[n_in + 3 + n_cast:]
    if scalar_decay:
        (q_ref, kf_ref, kb_ref, v_ref, lg_ref, gate_ref, nw_ref, s0f_ref, s0b_ref,
         y_ref, sf_ref, sb_ref, o_ref) = refs
    else:
        (q_ref, kf_ref, kb_ref, v_ref, lff_ref, lfb_ref, gate_ref, nw_ref, s0f_ref, s0b_ref,
         y_ref, sf_ref, sb_ref, o_ref) = refs
    c = chunk
    bf = jnp.bfloat16
    dk = q_ref.shape[-1]
    dv = o_ref.shape[-1]
    row = lax.broadcasted_iota(jnp.int32, (c, c), 0)
    col = lax.broadcasted_iota(jnp.int32, (c, c), 1)
    same_blk = (row // SUB) == (col // SUB)
    tri = {False: (col <= row), True: (col >= row)}
    dmask = {r: same_blk & t for r, t in tri.items()}
    k_refs = {False: kf_ref, True: kb_ref}
    st_refs = {False: sf_ref, True: sb_ref}
    sf_ref[...] = s0f_ref[...]
    sb_ref[...] = s0b_ref[...]
    if scalar_decay:
        pos = lax.broadcasted_iota(jnp.int32, (c, dk), 0).astype(jnp.float32)
        dist = (row - col).astype(jnp.float32)
        pair_decay, q_edge, k_edge, chunk_decay = {}, {}, {}, {}
        for hd in range(heads):
            for rev in (False, True):
                lg = lg_ref[int(rev), hd, 0:1, :]
                to_entry = (c - pos) if rev else (pos + 1.0)
                to_exit = pos if rev else (c - 1.0 - pos)
                signed = -dist if rev else dist
                pair_decay[hd, rev] = jnp.where(tri[rev], jnp.exp(lg[:, :c] * jnp.maximum(signed, 0.0)), 0.0)
                q_edge[hd, rev] = jnp.exp(lg * to_entry)
                k_edge[hd, rev] = jnp.exp(lg * to_exit)
                chunk_decay[hd, rev] = jnp.exp(lg * float(c))

    def step(n, accumulate):
        units = []
        for hd in range(heads):
            for u in range(group):
                for rev in (False, True):
                    chunk = n * group + u
                    if rev:
                        chunk = n_chunks - 1 - chunk
                    units.append((hd, rev, pl.ds(pl.multiple_of(chunk * c, c), c)))
        vs = [v_ref[hd, rows, :].astype(bf) for hd, _, rows in units]
        if scalar_decay:
            qs = [q_ref[hd, rows, :] for hd, _, rows in units]
            ks = [k_refs[rev][hd, rows, :] for hd, rev, rows in units]
            ops = [dict(q_state=(q * q_edge[hd, rev]).astype(bf), k_state=(k * k_edge[hd, rev]).astype(bf),
                        decay=chunk_decay[hd, rev]) for q, k, (hd, rev, _) in zip(qs, ks, units)]
            qk = [lax.dot_general(q.astype(bf), k.astype(bf), _NT, preferred_element_type=jnp.float32)
                  for q, k in zip(qs, ks)]
            scores = [(s * pair_decay[hd, rev]).astype(bf) for s, (hd, rev, _) in zip(qk, units)]
        else:
            lfs = [(lfb_ref if rev else lff_ref)[hd, rows, :] for hd, rev, rows in units]
            css = [_chunk_running_sum(lf, rev) for lf, (_, rev, _) in zip(lfs, units)]
            ops = [_chunk_operands(q_ref[hd, rows, :], k_refs[rev][hd, rows, :], cs, rev)
                   for cs, (hd, rev, rows) in zip(css, units)]
            diag = [lax.dot_general(p["qd"], p["kd"], _NT, preferred_element_type=jnp.float32) for p in ops]
            cross = [lax.dot_general(p["q_cat"], p["k_cat"], _NT, preferred_element_type=jnp.float32) for p in ops]
            scores = [(jnp.where(dmask[rev], d, 0.0) + x).astype(bf)
                      for d, x, (_, rev, _) in zip(diag, cross, units)]
        kv = [lax.dot_general(v, p["k_state"], _TN, preferred_element_type=jnp.float32) for v, p in zip(vs, ops)]
        intra = [jnp.dot(s, v, preferred_element_type=jnp.float32) for s, v in zip(scores, vs)]
        for hd in range(heads):
            for rev in (False, True):
                st = st_refs[rev][hd]
                for i, (h_i, r, rows) in enumerate(units):
                    if (h_i, r) != (hd, rev):
                        continue
                    o = intra[i] + lax.dot_general(ops[i]["q_state"], st.astype(bf), _NT,
                                                   preferred_element_type=jnp.float32)
                    st = st * ops[i]["decay"] + kv[i]
                    if accumulate:
                        o_ref[hd, rows, :] += o
                    else:
                        o_ref[hd, rows, :] = o
                st_refs[rev][hd] = st

    steps = n_chunks // group

    def first(n, carry):
        step(n, False)
        return carry

    def second(n, carry):
        step(n, True)
        return carry

    lax.fori_loop(0, steps // 2, first, 0)
    lax.fori_loop(steps // 2, steps, second, 0)

    tr = min(SCAN_EPILOGUE_ROWS, n_chunks * c)

    def epilogue(i, carry):
        rows = pl.ds(pl.multiple_of(i * tr, tr), tr)
        for hd in range(heads):
            o = o_ref[hd, rows, :]
            g = gate_ref[hd, rows, :]
            y = o * lax.rsqrt(jnp.mean(o * o, axis=-1, keepdims=True) + RMS_EPS) * nw_ref[...]
            y_ref[rows, hd * dv:(hd + 1) * dv] = (y * (g * jax.nn.sigmoid(g))).astype(y_ref.dtype)
        return carry

    lax.fori_loop(0, n_chunks * c // tr, epilogue, 0)


def _bidir_scan(q, k_f, k_b, v, lf_f, lf_b, gate, norm_w, s0_f, s0_b, n_heads, log_decay=None, cast=None):
    b, _, l, dk = q.shape
    dv = v.shape[-1]
    scalar_decay = log_decay is not None
    chunk = RETENTION_CHUNK if scalar_decay and l % (2 * RETENTION_CHUNK) == 0 else CHUNK
    n_chunks = l // chunk
    group = min(SCAN_GROUP, n_chunks // 2)
    heads = min(n_heads, SCAN_GROUP // group)
    assert l % chunk == 0 and n_chunks % (2 * group) == 0 and n_heads % heads == 0 and chunk <= dk
    seq_k = pl.BlockSpec((None, heads, l, dk), lambda bi, hi: (bi, hi, 0, 0))
    seq_v = pl.BlockSpec((None, heads, l, dv), lambda bi, hi: (bi, hi, 0, 0))
    out_y = pl.BlockSpec((None, l, heads * dv), lambda bi, hi: (bi, 0, hi))
    state = pl.BlockSpec((None, heads, dv, dk), lambda bi, hi: (bi, hi, 0, 0))
    if scalar_decay:
        lg = jnp.broadcast_to(log_decay.astype(jnp.float32)[:, :, None, None], (2, n_heads, 8, dk))
        decay_args = (lg,)
        decay_specs = [pl.BlockSpec((2, heads, 8, dk), lambda bi, hi: (0, hi, 0, 0))]
    else:
        decay_args = (lf_f, lf_b)
        decay_specs = [seq_k, seq_k]
    h_steps = n_heads // heads
    cast_args, cast_in_specs, cast_out_specs, cast_out_shape = [], [], [], []
    plain_cast = None
    if cast is not None:
        layer, tensors = cast
        steps = b * h_steps

        def fits(w):
            _, e, r, _ = w.shape
            return steps % e == 0 and r % (steps // e) == 0 and (r // (steps // e)) % BF16_SUBLANES == 0

        if not all(fits(w) for w in tensors):
            plain_cast = tuple(w[layer].astype(jnp.bfloat16) for w in tensors)
            tensors = ()
        for w in tensors:
            _, e, r, c = w.shape
            slabs = steps // e
            cast_args.append(w)
            cast_in_specs.append(pl.BlockSpec(
                (None, None, r // slabs, c),
                lambda bi, hi, slabs=slabs: (layer, (bi * h_steps + hi) // slabs, (bi * h_steps + hi) % slabs, 0)))
            cast_out_specs.append(pl.BlockSpec(
                (None, r // slabs, c),
                lambda bi, hi, slabs=slabs: ((bi * h_steps + hi) // slabs, (bi * h_steps + hi) % slabs, 0)))
            cast_out_shape.append(jax.ShapeDtypeStruct((e, r, c), jnp.bfloat16))
    res = pl.pallas_call(
        functools.partial(_bidir_scan_body, chunk=chunk, n_chunks=n_chunks, group=group, heads=heads,
                          scalar_decay=scalar_decay, n_cast=len(cast_args)),
        grid=(b, h_steps),
        in_specs=[seq_k, seq_k, seq_k, seq_v] + decay_specs + [
            seq_v, pl.BlockSpec((1, dv), lambda bi, hi: (0, 0)), state, state] + cast_in_specs,
        out_specs=[out_y, state, state] + cast_out_specs,
        out_shape=[jax.ShapeDtypeStruct((b, l, n_heads * dv), jnp.bfloat16),
                   jax.ShapeDtypeStruct((b, n_heads, dv, dk), jnp.float32),
                   jax.ShapeDtypeStruct((b, n_heads, dv, dk), jnp.float32)] + cast_out_shape,
        scratch_shapes=[pltpu.VMEM((heads, l, dv), jnp.float32)],
        compiler_params=pltpu.CompilerParams(
            dimension_semantics=("parallel", "parallel"),
            vmem_limit_bytes=VMEM_LIMIT_BYTES),
        name="bidir_scan",
    )(q, k_f, k_b, v, *decay_args, gate, norm_w.astype(jnp.float32).reshape(1, dv), s0_f, s0_b, *cast_args)
    if cast is None:
        return tuple(res)
    return (*res[:3], plain_cast if plain_cast is not None else tuple(res[3:]))


def _log_sigmoid(x):
    return jnp.minimum(x, 0.0) - jnp.log(1.0 + jnp.exp(-jnp.abs(x)))


def _store_heads(ref, t):
    n_heads, _, dh = ref.shape
    for hh in range(n_heads):
        ref[hh] = t[:, hh * dh:(hh + 1) * dh].astype(ref.dtype)


def _modulated_bf16(x_ref, sc_ref, sh_ref):
    return (x_ref[...] * (1.0 + sc_ref[...]) + sh_ref[...]).astype(jnp.bfloat16)


def _rope_heads(t, cos2, sin2, n_heads):
    dk = t.shape[-1] // n_heads
    out = []
    for hh in range(n_heads):
        th = t[:, hh * dk:(hh + 1) * dk]
        out.append(th * cos2 + pltpu.roll(th, dk // 2, axis=1) * sin2)
    return jnp.concatenate(out, axis=-1)


def _even_in_body(*refs, use_rope):
    if use_rope:
        x_ref, sc_ref, sh_ref, w_ref, lbf_ref, omlb_ref, cos_ref, sin_ref = refs[:8]
    else:
        x_ref, sc_ref, sh_ref, w_ref, lbf_ref, omlb_ref = refs[:6]
    (qa_ref, kaf_ref, kab_ref, lff_ref, lfb_ref, ia_ref, ga_ref, qb_ref, kb_ref, vb_ref, gb_ref) = refs[-11:]
    h = _modulated_bf16(x_ref, sc_ref, sh_ref)
    offs = np.cumsum((0,) + EVEN_SIZES).tolist()

    def seg(i):
        return jnp.dot(h, w_ref[:, offs[i]:offs[i + 1]], preferred_element_type=jnp.float32)

    _store_heads(qa_ref, seg(0) * HA_DK ** -0.5)
    for d, (lf_ref, key_ref) in enumerate(((lff_ref, kaf_ref), (lfb_ref, kab_ref))):
        z = seg(1 + d)
        e = jnp.exp(-jnp.abs(z))
        big = 1.0 / (1.0 + e)
        small = e * big
        pos = z >= 0.0
        sig_neg = jnp.where(pos, small, big)
        _store_heads(lf_ref, jnp.log(jnp.where(pos, big, small) + lbf_ref[d:d + 1, :] * sig_neg))
        _store_heads(key_ref, omlb_ref[d:d + 1, :] * sig_neg)
    _store_heads(ia_ref, seg(3))
    _store_heads(ga_ref, seg(4))
    qb = seg(5) * RB_DK ** -0.5
    kb = seg(6)
    if use_rope:
        qb = _rope_heads(qb, cos_ref[...], sin_ref[...], RB_HEADS)
        kb = _rope_heads(kb, cos_ref[...], sin_ref[...], RB_HEADS)
    _store_heads(qb_ref, qb)
    _store_heads(kb_ref, kb)
    _store_heads(vb_ref, seg(7))
    _store_heads(gb_ref, seg(8))


def _row_tiled_call(body, name, x, sc, sh, consts, seq_inputs, n_heads, out_widths, out_dtypes):
    b, l, d = x.shape
    tm = min(l, PROJ_ROWS)
    assert l % tm == 0

    def mod_spec(m):
        if m.shape[0] == b:
            return pl.BlockSpec((None, 1, d), lambda bi, ri: (bi, 0, 0))
        return pl.BlockSpec((None, 1, d), lambda bi, ri: (0, 0, 0))

    def const_spec(a):
        return pl.BlockSpec(a.shape, lambda bi, ri, nd=a.ndim: (0,) * nd)

    return pl.pallas_call(
        body,
        grid=(b, l // tm),
        in_specs=[pl.BlockSpec((None, tm, d), lambda bi, ri: (bi, ri, 0)), mod_spec(sc), mod_spec(sh)]
        + [const_spec(a) for a in consts]
        + [pl.BlockSpec((tm, a.shape[-1]), lambda bi, ri: (ri, 0)) for a in seq_inputs],
        out_specs=[pl.BlockSpec((None, n_heads, tm, w // n_heads), lambda bi, ri: (bi, 0, ri, 0))
                   for w in out_widths],
        out_shape=[jax.ShapeDtypeStruct((b, n_heads, l, w // n_heads), dt)
                   for w, dt in zip(out_widths, out_dtypes)],
        compiler_params=pltpu.CompilerParams(
            dimension_semantics=("parallel", "parallel"),
            vmem_limit_bytes=VMEM_LIMIT_BYTES),
        name=name,
    )(x, sc, sh, *consts, *seq_inputs)


def _even_mixer(x, sc, sh, w_in, lb, ha_norm_w, rb_log_gamma, rope, states, cast=None):
    f32, bf = jnp.float32, jnp.bfloat16
    assert HA_HEADS == RB_HEADS
    consts = [w_in.astype(bf), jnp.maximum(lb, LB_FLOOR), 1.0 - lb]
    seq = []
    if rope is not None:
        cos, sin = rope
        seq = [jnp.concatenate([cos, cos], axis=-1), jnp.concatenate([-sin, sin], axis=-1)]
    qa, ka_f, ka_b, lf_f, lf_b, ia, ga, qb, kb, vb, gb = _row_tiled_call(
        functools.partial(_even_in_body, use_rope=rope is not None), "even_in_proj", x, sc, sh, consts, seq,
        HA_HEADS, (HA_KEY,) * 5 + (HA_VAL, HA_VAL, RB_KEY, RB_KEY, RB_VAL, RB_VAL),
        (f32,) * 5 + (bf, f32, f32, f32, bf, f32))
    ya, sa_f, sa_b, *casted = _bidir_scan(qa, ka_f, ka_b, ia, lf_f, lf_b, ga, ha_norm_w, states[0][0], states[0][1],
                                          HA_HEADS, cast=cast)
    yb, sb_f, sb_b = _bidir_scan(qb, kb, kb, vb, None, None, gb, jnp.ones((RB_DV,), f32),
                                 states[1][0], states[1][1], RB_HEADS, log_decay=rb_log_gamma)
    return (ya, yb), ((sa_f, sa_b), (sb_f, sb_b)), (casted[0] if casted else None)


def _odd_in_body(x_ref, sc_ref, sh_ref, w_ref, w2_ref, b2_ref, q_ref, k_ref, v_ref, g_ref, lff_ref, lfb_ref):
    h = _modulated_bf16(x_ref, sc_ref, sh_ref)
    offs = np.cumsum((0,) + ODD_SIZES[:4]).tolist()

    def seg(lo, hi):
        return jnp.dot(h, w_ref[:, lo:hi], preferred_element_type=jnp.float32)

    _store_heads(q_ref, seg(offs[0], offs[1]) * GC_DK ** -0.5)
    _store_heads(k_ref, seg(offs[1], offs[2]))
    _store_heads(v_ref, seg(offs[2], offs[3]))
    _store_heads(g_ref, seg(offs[3], offs[4]))
    lr = seg(offs[4], w_ref.shape[-1]).astype(jnp.bfloat16)
    for d, lf_ref in enumerate((lff_ref, lfb_ref)):
        zz = jnp.dot(lr, w2_ref[d], preferred_element_type=jnp.float32) + b2_ref[d:d + 1, :]
        _store_heads(lf_ref, _log_sigmoid(zz) / GC_TAU)


def _odd_mixer(x, sc, sh, w_in, w2, b2, norm_w, states):
    f32, bf = jnp.float32, jnp.bfloat16
    lane = 128
    pad = lane - 2 * GC_RANK
    w_pad = jnp.pad(w_in, ((0, 0), (0, pad))).astype(bf)
    w2_pad = jnp.stack([jnp.pad(w2[0], ((0, lane - GC_RANK), (0, 0))),
                        jnp.pad(w2[1], ((GC_RANK, lane - 2 * GC_RANK), (0, 0)))]).astype(bf)
    q, k, v, g, lf_f, lf_b = _row_tiled_call(
        _odd_in_body, "odd_in_proj", x, sc, sh, [w_pad, w2_pad, b2], [], GC_HEADS,
        (GC_KEY, GC_KEY, GC_VAL, GC_VAL, GC_KEY, GC_KEY), (f32, f32, bf, f32, f32, f32))
    y, s_f, s_b = _bidir_scan(q, k, k, v, lf_f, lf_b, g, norm_w, states[0], states[1], GC_HEADS)
    return (y,), (s_f, s_b)


def _mixer_out_body(*refs, n_y):
    y_refs = refs[:n_y]
    (wo_ref, x_ref, g1_ref, lnw_ref, lnb_ref, sc2_ref, sh2_ref, wr_ref, x1_ref, h2_ref, aff_ref) = refs[n_y:]
    mix = None
    lo = 0
    for y_ref in y_refs:
        hi = lo + y_ref.shape[-1]
        part = jnp.dot(y_ref[...], wo_ref[lo:hi, :], preferred_element_type=jnp.float32)
        mix = part if mix is None else mix + part
        lo = hi
    z = DN_ALPHA * x_ref[...] + g1_ref[...] * mix
    mu = jnp.mean(z, axis=-1, keepdims=True)
    zc = z - mu
    var = jnp.mean(zc * zc, axis=-1, keepdims=True)
    x1 = zc * lax.rsqrt(var + LN_EPS) * lnw_ref[...] + lnb_ref[...]
    x1_ref[...] = x1
    h2 = (x1 * (1.0 + sc2_ref[...]) + sh2_ref[...]).astype(jnp.bfloat16)
    h2_ref[...] = h2
    logits = lax.dot_general(wr_ref[...], h2, _NT, preferred_element_type=jnp.float32)
    ex = jnp.exp(logits - jnp.max(logits, axis=0, keepdims=True))
    aff_ref[...] = ex / jnp.sum(ex, axis=0, keepdims=True)


def _mixer_out(ys, w_out, x, g1, ln_w, ln_b, sc2, sh2, w_router):
    b, l, d = x.shape
    e = w_router.shape[-1]
    tm = min(l, MIXER_OUT_ROWS)
    assert l % tm == 0

    def mod_spec(m):
        if m.shape[0] == b:
            return pl.BlockSpec((None, 1, d), lambda bi, ri: (bi, 0, 0))
        return pl.BlockSpec((None, 1, d), lambda bi, ri: (0, 0, 0))

    def const_spec(a):
        return pl.BlockSpec(a.shape, lambda bi, ri, nd=a.ndim: (0,) * nd)

    def rows(w):
        return pl.BlockSpec((None, tm, w), lambda bi, ri: (bi, ri, 0))

    consts_a = [w_out.astype(jnp.bfloat16)]
    consts_b = [ln_w.reshape(1, d), ln_b.reshape(1, d)]
    wr_t = w_router.T.astype(jnp.bfloat16)
    return pl.pallas_call(
        functools.partial(_mixer_out_body, n_y=len(ys)),
        grid=(b, l // tm),
        in_specs=[rows(y.shape[-1]) for y in ys] + [const_spec(consts_a[0]), rows(d), mod_spec(g1)]
        + [const_spec(a) for a in consts_b] + [mod_spec(sc2), mod_spec(sh2), const_spec(wr_t)],
        out_specs=[rows(d), rows(d), pl.BlockSpec((None, e, tm), lambda bi, ri: (bi, 0, ri))],
        out_shape=[jax.ShapeDtypeStruct((b, l, d), jnp.float32), jax.ShapeDtypeStruct((b, l, d), jnp.bfloat16),
                   jax.ShapeDtypeStruct((b, e, l), jnp.float32)],
        compiler_params=pltpu.CompilerParams(
            dimension_semantics=("parallel", "parallel"),
            vmem_limit_bytes=VMEM_LIMIT_BYTES),
        name="mixer_out",
    )(*ys, *consts_a, x, g1, *consts_b, sc2, sh2, wr_t)


def _expert_ffn_body(*refs, cast_next):
    if cast_next:
        (x_ref, gate_ref, wg_ref, wu_ref, wd_ref, ng_ref, nu_ref, nd_ref,
         out_ref, og_ref, ou_ref, od_ref, hid_ref) = refs
        og_ref[...] = ng_ref[...].astype(og_ref.dtype)
        ou_ref[...] = nu_ref[...].astype(ou_ref.dtype)
        od_ref[...] = nd_ref[...].astype(od_ref.dtype)
    else:
        x_ref, gate_ref, wg_ref, wu_ref, wd_ref, out_ref, hid_ref = refs
    tb, cap, d = x_ref.shape
    ff = wg_ref.shape[-1]
    x = x_ref[...].reshape(tb * cap, d)
    for j in range(ff // FFN_FF_TILE):
        cols = slice(j * FFN_FF_TILE, (j + 1) * FFN_FF_TILE)
        g = jnp.dot(x, wg_ref[:, cols], preferred_element_type=jnp.float32)
        u = jnp.dot(x, wu_ref[:, cols], preferred_element_type=jnp.float32)
        hid_ref[:, cols] = (g * jax.nn.sigmoid(g) * u).astype(jnp.bfloat16)
    y = jnp.dot(hid_ref[...], wd_ref[...], preferred_element_type=jnp.float32)
    out_ref[...] = (y.reshape(tb, cap, d) * gate_ref[...]).astype(out_ref.dtype)


def _expert_ffn(xs, gate, w_gate, w_up, w_down, next_layer=None, next_weights=None):
    b, e, cap, d = xs.shape
    ff = w_gate.shape[-1]
    tb = max(1, min(b, FFN_ROWS_PER_STEP // cap))
    steps = b // tb
    assert b % tb == 0 and ff % FFN_FF_TILE == 0
    cast_next = next_weights is not None
    in_specs = [
        pl.BlockSpec((tb, None, cap, d), lambda ei, bi: (bi, ei, 0, 0)),
        pl.BlockSpec((tb, None, cap, 1), lambda ei, bi: (bi, ei, 0, 0)),
        pl.BlockSpec((None, d, ff), lambda ei, bi: (ei, 0, 0)),
        pl.BlockSpec((None, d, ff), lambda ei, bi: (ei, 0, 0)),
        pl.BlockSpec((None, ff, d), lambda ei, bi: (ei, 0, 0)),
    ]
    out_specs = [pl.BlockSpec((tb, None, cap, d), lambda ei, bi: (bi, ei, 0, 0))]
    out_shape = [jax.ShapeDtypeStruct((b, e, cap, d), jnp.bfloat16)]
    args = [xs, gate, w_gate, w_up, w_down]
    if cast_next:
        d_slab, f_slab = d // steps, ff // steps
        assert d % steps == 0 and ff % steps == 0 and d_slab % BF16_SUBLANES == 0 and f_slab % BF16_SUBLANES == 0
        in_specs += [
            pl.BlockSpec((None, None, d_slab, ff), lambda ei, bi: (next_layer, ei, bi, 0)),
            pl.BlockSpec((None, None, d_slab, ff), lambda ei, bi: (next_layer, ei, bi, 0)),
            pl.BlockSpec((None, None, f_slab, d), lambda ei, bi: (next_layer, ei, bi, 0)),
        ]
        out_specs += [
            pl.BlockSpec((None, d_slab, ff), lambda ei, bi: (ei, bi, 0)),
            pl.BlockSpec((None, d_slab, ff), lambda ei, bi: (ei, bi, 0)),
            pl.BlockSpec((None, f_slab, d), lambda ei, bi: (ei, bi, 0)),
        ]
        out_shape += [jax.ShapeDtypeStruct((e, d, ff), jnp.bfloat16), jax.ShapeDtypeStruct((e, d, ff), jnp.bfloat16),
                      jax.ShapeDtypeStruct((e, ff, d), jnp.bfloat16)]
        args += list(next_weights)
    res = pl.pallas_call(
        functools.partial(_expert_ffn_body, cast_next=cast_next),
        grid=(e, steps),
        in_specs=in_specs,
        out_specs=out_specs,
        out_shape=out_shape,
        scratch_shapes=[pltpu.VMEM((tb * cap, ff), jnp.bfloat16)],
        compiler_params=pltpu.CompilerParams(
            dimension_semantics=("parallel", "parallel"),
            vmem_limit_bytes=VMEM_LIMIT_BYTES),
        name="expert_ffn",
    )(*args)
    return (res[0], tuple(res[1:])) if cast_next else (res[0], None)


def _layer_norm_rows(z, w, b):
    mu = jnp.mean(z, axis=-1, keepdims=True)
    zc = z - mu
    var = jnp.mean(zc * zc, axis=-1, keepdims=True)
    return zc * lax.rsqrt(var + LN_EPS) * w + b


def _combine_ln_body(lohi_ref, idx_ref, idx_all_ref, ys_ref, x_ref, g_ref, w_ref, b_ref, out_ref, ffn_ref, *,
                     window):
    tr, d = x_ref.shape
    e, cap, _ = ys_ref.shape
    bi, ri = pl.program_id(0), pl.program_id(1)
    n_bounds = pl.num_programs(1) + 1
    tokens = lax.broadcasted_iota(jnp.int32, (tr, 1), 0) + ri * tr

    def all_slots():
        onehot = jnp.where(tokens == idx_all_ref[...], 1.0, 0.0).astype(jnp.bfloat16)
        return jnp.dot(onehot, ys_ref[...].reshape(e * cap, d), preferred_element_type=jnp.float32)

    def finish(ffn):
        z = DN_ALPHA * x_ref[...] + g_ref[...] * ffn
        out_ref[...] = _layer_norm_rows(z, w_ref[...], b_ref[...])

    if window == cap:
        finish(all_slots())
        return
    per_dot = CONTRACT_DEPTH // window
    ffn = jnp.zeros((tr, d), jnp.float32)
    overflow = jnp.int32(0)
    for e0 in range(0, e, per_dot):
        wins, rows = [], []
        for ei in range(e0, e0 + per_dot):
            lo = lohi_ref[bi, ei * n_bounds + ri]
            hi = lohi_ref[bi, ei * n_bounds + ri + 1]
            start = jnp.minimum((lo // BF16_SUBLANES) * BF16_SUBLANES, cap - window)
            overflow = overflow + (hi > start + window).astype(jnp.int32)
            wins.append(pltpu.roll(idx_ref[ei], (cap - start) % cap, axis=1)[:, :window])
            rows.append(ys_ref[ei, pl.ds(pl.multiple_of(start, BF16_SUBLANES), window), :])
        onehot = jnp.where(tokens == jnp.concatenate(wins, axis=1), 1.0, 0.0).astype(jnp.bfloat16)
        ffn = ffn + jnp.dot(onehot, jnp.concatenate(rows, axis=0), preferred_element_type=jnp.float32)
    ffn_ref[...] = ffn

    @pl.when(overflow > 0)
    def _():
        ffn_ref[...] = all_slots()

    finish(ffn_ref[...])


def _combine_ln(ys, idx, x, g, w, b):
    bsz, e, cap, d = ys.shape
    n = x.shape[1]
    tr = min(n, COMBINE_ROWS)
    window = min(cap, COMBINE_WINDOW)
    assert n % tr == 0 and CONTRACT_DEPTH % window == 0 and e % (CONTRACT_DEPTH // window) == 0
    assert cap % BF16_SUBLANES == 0
    per_sample = g.shape[0] == bsz
    bounds = jnp.arange(0, n + 1, tr, dtype=jnp.int32)
    lohi = jnp.sum(idx[..., None] < bounds, axis=2, dtype=jnp.int32).reshape(bsz, e * (n // tr + 1))
    grid_spec = pltpu.PrefetchScalarGridSpec(
        num_scalar_prefetch=1,
        grid=(bsz, n // tr),
        in_specs=[
            pl.BlockSpec((None, e, 1, cap), lambda bi, ri, lohi: (bi, 0, 0, 0)),
            pl.BlockSpec((None, 1, e * cap), lambda bi, ri, lohi: (bi, 0, 0)),
            pl.BlockSpec((None, e, cap, d), lambda bi, ri, lohi: (bi, 0, 0, 0)),
            pl.BlockSpec((None, tr, d), lambda bi, ri, lohi: (bi, ri, 0)),
            pl.BlockSpec((None, 1, d), (lambda bi, ri, lohi: (bi, 0, 0)) if per_sample
                         else (lambda bi, ri, lohi: (0, 0, 0))),
            pl.BlockSpec((1, d), lambda bi, ri, lohi: (0, 0)),
            pl.BlockSpec((1, d), lambda bi, ri, lohi: (0, 0)),
        ],
        out_specs=pl.BlockSpec((None, tr, d), lambda bi, ri, lohi: (bi, ri, 0)),
        scratch_shapes=[pltpu.VMEM((tr, d), jnp.float32)],
    )
    return pl.pallas_call(
        functools.partial(_combine_ln_body, window=window),
        grid_spec=grid_spec,
        out_shape=jax.ShapeDtypeStruct((bsz, n, d), jnp.float32),
        compiler_params=pltpu.CompilerParams(
            dimension_semantics=("parallel", "parallel"),
            vmem_limit_bytes=VMEM_LIMIT_BYTES),
        name="combine_ln",
    )(lohi, idx.reshape(bsz, e, 1, cap), idx.reshape(bsz, 1, e * cap), ys, x, g, w.reshape(1, d), b.reshape(1, d))


def _ec_ffn_ln(x, h, aff, g, experts, ln_w, ln_b, next_layer=None, next_weights=None):
    n = x.shape[1]
    cap = EC_CAPACITY_FACTOR * n // N_EXPERTS
    gate, idx = lax.top_k(aff, cap)
    idx, gate = lax.sort((idx, gate), dimension=2, num_keys=1)
    xs = jax.vmap(lambda hb, ib: hb[ib])(h, idx)
    ys, next_experts = _expert_ffn(xs, gate[..., None], *experts, next_layer=next_layer, next_weights=next_weights)
    return _combine_ln(ys, idx, x, g, ln_w, ln_b), next_experts


def kernel(x, c, ctx, c_ctx, ada_w, ada_b, ln_w, ln_b, even_w_in, even_w_out, ha_lb, ha_norm,
           rb_decay, odd_w_in, odd_w_out, gc_w2, gc_b2, gc_norm, router_w, exp_w_gate, exp_w_up,
           exp_w_down):
    n_lat = x.shape[1]
    rows = n_lat // GRID_W
    rope = _axial_rope(rows)
    b_ctx = ctx.shape[0]
    lb_p = jax.nn.softmax(ha_lb, axis=0)
    lb_all = jnp.cumsum(lb_p, axis=0) - lb_p[0]
    cond_lat = jax.nn.silu(c)
    cond_ctx = jax.nn.silu(c_ctx)
    stacked_experts = (exp_w_gate, exp_w_up, exp_w_down)
    experts = None

    for l in range(DEPTH):
        last = l == DEPTH - 1
        mod_lat = (cond_lat @ ada_w[l] + ada_b[l])[:, None, :]
        mod_ctx = (cond_ctx @ ada_w[l] + ada_b[l])[None, None, :]
        sh1, sc1, g1, sh2, sc2, g2 = jnp.split(mod_lat, 6, axis=-1)
        csh1, csc1, cg1, csh2, csc2, cg2 = jnp.split(mod_ctx, 6, axis=-1)

        j = l // 2
        if l % 2 == 0:
            z_a = jnp.zeros((b_ctx, HA_HEADS, HA_DV, HA_DK), jnp.float32)
            z_b = jnp.zeros((b_ctx, RB_HEADS, RB_DV, RB_DK), jnp.float32)
            log_gamma = jax.nn.log_sigmoid(rb_decay[j])
            w_out = even_w_out[j]
            y_ctx, st, _ = _even_mixer(ctx, csc1, csh1, even_w_in[j], lb_all[j], ha_norm[j], log_gamma, None,
                                       ((z_a, z_a), (z_b, z_b)))
            y_lat, _, casted = _even_mixer(x, sc1, sh1, even_w_in[j], lb_all[j], ha_norm[j], log_gamma, rope, st,
                                           cast=(0, stacked_experts) if l == 0 else None)
            if l == 0:
                experts = casted
        else:
            z_c = jnp.zeros((b_ctx, GC_HEADS, GC_DV, GC_DK), jnp.float32)
            w_out = odd_w_out[j]
            y_ctx, st = _odd_mixer(ctx, csc1, csh1, odd_w_in[j], gc_w2[j], gc_b2[j], gc_norm[j], (z_c, z_c))
            y_lat, _ = _odd_mixer(x, sc1, sh1, odd_w_in[j], gc_w2[j], gc_b2[j], gc_norm[j], st)

        x, h2, aff = _mixer_out(y_lat, w_out, x, g1, ln_w[l, 0], ln_b[l, 0], sc2, sh2, router_w[l])
        x, next_experts = _ec_ffn_ln(x, h2, aff, g2, experts, ln_w[l, 1], ln_b[l, 1],
                                     next_layer=None if last else l + 1,
                                     next_weights=None if last else stacked_experts)
        if not last:
            ctx, h2, aff = _mixer_out(y_ctx, w_out, ctx, cg1, ln_w[l, 0], ln_b[l, 0], csc2, csh2, router_w[l])
            ctx, _ = _ec_ffn_ln(ctx, h2, aff, cg2, experts, ln_w[l, 1], ln_b[l, 1])
            experts = next_experts
    return x
```
